```python
import jax
import jax.numpy as jnp
from jax import lax
import numpy as np

D_MODEL = 2048
BATCH = 4
SEQ = 8192
DEPTH = 1

PLE_DIM = 256
EPS = 1e-6

ML_HEADS = 8
ML_DQK = 128
ML_DV = 256
ML_CHUNK = 64
GATE_CAP = 15.0
FGATE_BIAS = 3.0

SW_HEADS = 32
SW_KV_HEADS = 4
SW_GROUP = SW_HEADS // SW_KV_HEADS
SW_HD = 64
WINDOW = 128
SW_BLOCK = WINDOW

N_EXPERTS = 32
TOP_K = 4
D_FF = D_MODEL
SWIGLU_LIMIT = 7.0
SWIGLU_ALPHA = 1.702
MOE_BLOCK = 128

ML_QK_W = ML_HEADS * ML_DQK
ML_V_W = ML_HEADS * ML_DV
SW_Q_W = SW_HEADS * SW_HD
SW_KV_W = SW_KV_HEADS * SW_HD
IN_SIZES = (ML_QK_W, ML_QK_W, ML_V_W, ML_V_W, ML_HEADS, ML_HEADS, SW_Q_W, SW_KV_W, SW_KV_W, D_MODEL, D_MODEL)
D_IN = sum(IN_SIZES)

kernel_name = 'hybrid_mlstm_swa_moe_block'


def rms_norm(x, w):
    xf = x.astype(jnp.float32)
    y = xf * lax.rsqrt(jnp.mean(xf * xf, axis=-1, keepdims=True) + EPS)
    return (y * w.astype(jnp.float32)).astype(x.dtype)


def alibi_slopes(n_heads):
    return 2.0 ** (-8.0 * jnp.arange(1, n_heads + 1, dtype=jnp.float32) / n_heads)


def mlstm_branch(q, k, v, o_pre, i_pre, f_pre, norm_w):
    f32 = jnp.float32
    B, S = q.shape[0], q.shape[1]
    nc = S // ML_CHUNK

    def chunks(a, d):
        return a.astype(f32).reshape(B, nc, ML_CHUNK, ML_HEADS, d).transpose(1, 0, 3, 2, 4)

    def gate_chunks(a):
        return a.reshape(B, nc, ML_CHUNK, ML_HEADS).transpose(1, 0, 3, 2)

    qc = chunks(q, ML_DQK) * (ML_DQK ** -0.5)
    kc = chunks(k, ML_DQK)
    vc = chunks(v, ML_DV)
    log_i = gate_chunks(GATE_CAP * jnp.tanh(i_pre.astype(f32) / GATE_CAP))
    log_f = gate_chunks(jax.nn.log_sigmoid(GATE_CAP * jnp.tanh(f_pre.astype(f32) / GATE_CAP)))
    causal = jnp.tril(jnp.ones((ML_CHUNK, ML_CHUNK), dtype=bool))

    def step(carry, xs):
        C, n, m = carry
        qb, kb, vb, li, lf = xs
        b = jnp.cumsum(lf, axis=-1)
        g = b[..., -1]
        log_d = jnp.where(causal, b[..., :, None] - b[..., None, :] + li[..., None, :], -jnp.inf)
        log_inter = b + m[..., None]
        m_row = jnp.maximum(log_inter, log_d.max(axis=-1))
        inter = jnp.exp(log_inter - m_row)
        s = jnp.einsum('bhid,bhjd->bhij', qb, kb) * jnp.exp(log_d - m_row[..., None])
        num = jnp.einsum('bhij,bhjv->bhiv', s, vb) + inter[..., None] * jnp.einsum('bhid,bhdv->bhiv', qb, C)
        den = s.sum(axis=-1) + inter * jnp.einsum('bhid,bhd->bhi', qb, n)
        h = num / jnp.maximum(jnp.abs(den), jnp.exp(-m_row))[..., None]
        log_w = g[..., None] - b + li
        m_new = jnp.maximum(g + m, log_w.max(axis=-1))
        w = jnp.exp(log_w - m_new[..., None])
        decay = jnp.exp(g + m - m_new)
        C_new = decay[..., None, None] * C + jnp.einsum('bhj,bhjd,bhjv->bhdv', w, kb, vb)
        n_new = decay[..., None] * n + jnp.einsum('bhj,bhjd->bhd', w, kb)
        return (C_new, n_new, m_new), h

    init = (jnp.zeros((B, ML_HEADS, ML_DQK, ML_DV), f32),
            jnp.zeros((B, ML_HEADS, ML_DQK), f32),
            jnp.zeros((B, ML_HEADS), f32))
    _, h = lax.scan(step, init, (qc, kc, vc, log_i, log_f))
    h = h.transpose(1, 0, 3, 2, 4).reshape(B, S, ML_HEADS, ML_DV)
    h = h * lax.rsqrt(jnp.mean(h * h, axis=-1, keepdims=True) + EPS) * norm_w.astype(f32).reshape(ML_HEADS, ML_DV)
    h = h.reshape(B, S, ML_V_W) * jax.nn.sigmoid(o_pre.astype(f32))
    return h.astype(q.dtype)


def swa_branch(q, k, v, q_norm_w, k_norm_w, sinks):
    f32 = jnp.float32
    B, S = q.shape[0], q.shape[1]
    nb = S // SW_BLOCK
    q = rms_norm(q.reshape(B, S, SW_HEADS, SW_HD), q_norm_w)
    k = rms_norm(k.reshape(B, S, SW_KV_HEADS, SW_HD), k_norm_w)
    v = v.reshape(B, S, SW_KV_HEADS, SW_HD)
    qb = q.reshape(B, nb, SW_BLOCK, SW_KV_HEADS, SW_GROUP, SW_HD)

    def band(a):
        a = a.reshape(B, nb, SW_BLOCK, SW_KV_HEADS, SW_HD)
        prev = jnp.pad(a[:, :-1], ((0, 0), (1, 0), (0, 0), (0, 0), (0, 0)))
        return jnp.concatenate([prev, a], axis=2)

    kb, vb = band(k), band(v)
    scores = jnp.einsum('bnqhgd,bnkhd->bnhgqk', qb, kb).astype(f32) * (SW_HD ** -0.5)
    q_pos = jnp.arange(SW_BLOCK)[:, None] + SW_BLOCK
    k_pos = jnp.arange(2 * SW_BLOCK)[None, :]
    dist = q_pos - k_pos
    in_window = (dist >= 0) & (dist < WINDOW)
    has_prev = (jnp.arange(nb) > 0)[:, None, None] | (k_pos >= SW_BLOCK)[None]
    mask = (in_window[None] & has_prev)[None, :, None, None]
    slopes = alibi_slopes(SW_HEADS).reshape(SW_KV_HEADS, SW_GROUP)
    scores = scores - slopes[:, :, None, None] * dist.astype(f32)
    scores = jnp.where(mask, scores, -jnp.inf)
    sink = sinks.astype(f32).reshape(SW_KV_HEADS, SW_GROUP, 1, 1)
    m = jnp.maximum(scores.max(axis=-1, keepdims=True), sink)
    e = jnp.exp(scores - m)
    probs = e / (e.sum(axis=-1, keepdims=True) + jnp.exp(sink - m))
    out = jnp.einsum('bnhgqk,bnkhd->bnqhgd', probs.astype(v.dtype), vb)
    return out.reshape(B, S, SW_Q_W)


def moe_ffn(xn, w_router, b_router, w_up, b_up, w_down, b_down):
    B, S, D = xn.shape
    T = B * S
    xt = xn.reshape(T, D)
    logits = (xt @ w_router + b_router).astype(jnp.float32)
    top_val, top_idx = lax.top_k(logits, TOP_K)
    gates = jax.nn.softmax(top_val, axis=-1).astype(xn.dtype)
    flat_e = top_idx.reshape(-1)
    flat_tok = jnp.repeat(jnp.arange(T, dtype=jnp.int32), TOP_K)
    flat_g = gates.reshape(-1)
    order = jnp.argsort(flat_e)
    e_sorted, tok_sorted, g_sorted = flat_e[order], flat_tok[order], flat_g[order]
    counts = jnp.bincount(flat_e, length=N_EXPERTS)
    starts = jnp.cumsum(counts) - counts
    padded = (counts + MOE_BLOCK - 1) // MOE_BLOCK * MOE_BLOCK
    pad_ends = jnp.cumsum(padded)
    pad_starts = pad_ends - padded
    dest = pad_starts[e_sorted] + (jnp.arange(T * TOP_K) - starts[e_sorted])
    P = T * TOP_K + N_EXPERTS * MOE_BLOCK
    n_blocks = P // MOE_BLOCK
    buf_tok = jnp.zeros((P,), jnp.int32).at[dest].set(tok_sorted)
    buf_g = jnp.zeros((P,), xn.dtype).at[dest].set(g_sorted)
    block_e = jnp.minimum(jnp.searchsorted(pad_ends, jnp.arange(n_blocks) * MOE_BLOCK, side='right'), N_EXPERTS - 1)

    def expert_block(args):
        tok, g, e = args
        hb = xt[tok] @ w_up[e] + b_up[e]
        h_glu, h_lin = jnp.split(hb, 2, axis=-1)
        h_glu = jnp.minimum(h_glu, SWIGLU_LIMIT)
        h_lin = jnp.clip(h_lin, -SWIGLU_LIMIT, SWIGLU_LIMIT)
        act = h_glu * jax.nn.sigmoid(SWIGLU_ALPHA * h_glu) * (h_lin + 1.0)
        return (act @ w_down[e] + b_down[e]) * g[:, None]

    ys = lax.map(expert_block, (buf_tok.reshape(n_blocks, MOE_BLOCK), buf_g.reshape(n_blocks, MOE_BLOCK), block_e))
    out = jnp.zeros_like(xt).at[buf_tok].add(ys.reshape(P, D))
    return out.reshape(B, S, D)


def setup_inputs(seed: int = 0) -> dict:
    key = jax.random.key(seed)
    ks = jax.random.split(key, 24)
    f32 = jnp.float32
    nrm = lambda k, shape, scale: jax.random.normal(k, shape, f32) * scale
    gain = lambda k, shape: 1.0 + 0.05 * jax.random.normal(k, shape, f32)
    return {
        'x': nrm(ks[0], (BATCH, SEQ, D_MODEL), 1.0),
        'p': nrm(ks[1], (DEPTH, BATCH, SEQ, PLE_DIM), 1.0),
        'norm_mix_w': gain(ks[2], (DEPTH, D_MODEL)),
        'w_in': nrm(ks[3], (DEPTH, D_MODEL, D_IN), D_MODEL ** -0.5),
        'ml_igate_b': nrm(ks[4], (DEPTH, ML_HEADS), 0.1),
        'ml_fgate_b': FGATE_BIAS + nrm(ks[5], (DEPTH, ML_HEADS), 0.1),
        'ml_norm_w': gain(ks[6], (DEPTH, ML_V_W)),
        'sw_q_norm_w': gain(ks[7], (DEPTH, SW_HD)),
        'sw_k_norm_w': gain(ks[8], (DEPTH, SW_HD)),
        'sw_sinks': nrm(ks[9], (DEPTH, SW_HEADS), 0.5),
        'w_branch_a': nrm(ks[10], (DEPTH, ML_V_W, D_MODEL), ML_V_W ** -0.5),
        'w_branch_b': nrm(ks[11], (DEPTH, SW_Q_W, D_MODEL), SW_Q_W ** -0.5),
        'w_out': nrm(ks[12], (DEPTH, D_MODEL, D_MODEL), D_MODEL ** -0.5),
        'norm_ffn_w': gain(ks[13], (DEPTH, D_MODEL)),
        'w_router': nrm(ks[14], (DEPTH, D_MODEL, N_EXPERTS), D_MODEL ** -0.5),
        'b_router': nrm(ks[15], (DEPTH, N_EXPERTS), 0.01),
        'w_expert_up': nrm(ks[16], (DEPTH, N_EXPERTS, D_MODEL, 2 * D_FF), D_MODEL ** -0.5),
        'b_expert_up': nrm(ks[17], (DEPTH, N_EXPERTS, 2 * D_FF), 0.02),
        'w_expert_down': nrm(ks[18], (DEPTH, N_EXPERTS, D_FF, D_MODEL), D_FF ** -0.5),
        'b_expert_down': nrm(ks[19], (DEPTH, N_EXPERTS, D_MODEL), 0.02),
        'norm_ple_w': gain(ks[20], (DEPTH, D_MODEL)),
        'w_ple_gate': nrm(ks[21], (DEPTH, D_MODEL, D_MODEL), D_MODEL ** -0.5),
        'w_ple_proj': nrm(ks[22], (DEPTH, PLE_DIM, D_MODEL), PLE_DIM ** -0.5),
    }


def reference(x, p, norm_mix_w, w_in, ml_igate_b, ml_fgate_b, ml_norm_w, sw_q_norm_w, sw_k_norm_w, sw_sinks,
              w_branch_a, w_branch_b, w_out, norm_ffn_w, w_router, b_router, w_expert_up, b_expert_up,
              w_expert_down, b_expert_down, norm_ple_w, w_ple_gate, w_ple_proj):
    split_points = np.cumsum(IN_SIZES)[:-1].tolist()
    h = x
    for i in range(DEPTH):
        xn = rms_norm(h, norm_mix_w[i])
        proj = xn @ w_in[i]
        (ml_q, ml_k, ml_v, ml_o, ml_i, ml_f, sw_q, sw_k, sw_v, g_a, g_b) = jnp.split(proj, split_points, axis=-1)
        y_a = mlstm_branch(ml_q, ml_k, ml_v, ml_o, ml_i + ml_igate_b[i], ml_f + ml_fgate_b[i], ml_norm_w[i]) @ w_branch_a[i]
        y_b = swa_branch(sw_q, sw_k, sw_v, sw_q_norm_w[i], sw_k_norm_w[i], sw_sinks[i]) @ w_branch_b[i]
        mixed = jax.nn.sigmoid(g_a) * y_a + jax.nn.sigmoid(g_b) * y_b
        h = h + mixed @ w_out[i]
        h = h + moe_ffn(rms_norm(h, norm_ffn_w[i]), w_router[i], b_router[i], w_expert_up[i], b_expert_up[i],
                        w_expert_down[i], b_expert_down[i])
        gate = jax.nn.sigmoid(rms_norm(h, norm_ple_w[i]) @ w_ple_gate[i])
        h = h + gate * (p[i] @ w_ple_proj[i])
    return h
```

```python
import functools

import jax
import jax.numpy as jnp
from jax import lax
from jax.experimental import pallas as pl
from jax.experimental.pallas import tpu as pltpu

F32 = jnp.float32
BF16 = jnp.bfloat16

EPS = 1e-6
ML_HEADS = 8
ML_DQK = 128
ML_DV = 256
GATE_CAP = 15.0
SW_HEADS = 32
SW_KV_HEADS = 4
SW_GROUP = SW_HEADS // SW_KV_HEADS
SW_HD = 64
WINDOW = 128
N_EXPERTS = 32
TOP_K = 4
SWIGLU_LIMIT = 7.0
SWIGLU_ALPHA = 1.702

LANES = 128
VMEM_LIMIT = 56 * 1024 * 1024

ML_CHUNK = 256
MOE_ROWS = 512
MOE_FF_CHUNK = 512
NEG_BIG = -1e30


def _cparams(*sem):
    return pltpu.CompilerParams(dimension_semantics=sem, vmem_limit_bytes=VMEM_LIMIT)


def _dot(a, b):
    return jnp.dot(a, b, preferred_element_type=F32)


def _dot_nt(a, b):
    return lax.dot_general(a, b, (((1,), (1,)), ((), ())), preferred_element_type=F32)


def _dot_tn(a, b):
    return lax.dot_general(a, b, (((0,), (0,)), ((), ())), preferred_element_type=F32)


def _dot_f32(a, b):
    return jnp.dot(a, b, preferred_element_type=F32, precision=lax.Precision.HIGHEST)


def _rms(x, w):
    return x * lax.rsqrt(jnp.mean(x * x, axis=-1, keepdims=True) + EPS) * w


def _sigmoid(x):
    return 1.0 / (1.0 + jnp.exp(-x))


def _in_proj_kernel(x_ref, nw_ref, w_ref, wg_ref, o_ref, g_ref, xn_ref):
    @pl.when(pl.program_id(1) == 0)
    def _():
        xn = _rms(x_ref[...], nw_ref[...]).astype(BF16)
        xn_ref[...] = xn
        g_ref[...] = _dot(xn, wg_ref[...])

    o_ref[...] = _dot(xn_ref[...], w_ref[...]).astype(o_ref.dtype)


def _in_proj(x2d, norm_w, w_main, w_gate):
    T, D = x2d.shape
    N = w_main.shape[1]
    tm = min(1024, T)
    tn = 1280
    return pl.pallas_call(
        _in_proj_kernel,
        grid=(T // tm, N // tn),
        in_specs=[pl.BlockSpec((tm, D), lambda i, j: (i, 0)),
                  pl.BlockSpec((1, D), lambda i, j: (0, 0)),
                  pl.BlockSpec((D, tn), lambda i, j: (0, j)),
                  pl.BlockSpec((D, LANES), lambda i, j: (0, 0))],
        out_specs=[pl.BlockSpec((tm, tn), lambda i, j: (i, j)),
                   pl.BlockSpec((tm, LANES), lambda i, j: (i, 0))],
        out_shape=[jax.ShapeDtypeStruct((T, N), BF16), jax.ShapeDtypeStruct((T, LANES), F32)],
        scratch_shapes=[pltpu.VMEM((tm, D), BF16)],
        compiler_params=_cparams("arbitrary", "arbitrary"),
        name="in_proj",
    )(x2d, norm_w, w_main, w_gate)


def _log_sigmoid(z):
    return jnp.minimum(z, 0.0) - jnp.log1p(jnp.exp(-jnp.abs(z)))


def _soft_cap(z):
    return GATE_CAP * jnp.tanh(z / GATE_CAP)


def _mlstm_kernel(q_ref, k_ref, v_ref, o_ref, g_ref, gt_ref, brow_ref, bcol_ref, nw_ref,
                  out_ref, c_ref, n_ref, m_ref):
    L = q_ref.shape[0]
    H = ML_HEADS

    @pl.when(pl.program_id(1) == 0)
    def _():
        c_ref[...] = jnp.zeros_like(c_ref)
        n_ref[...] = jnp.zeros_like(n_ref)
        m_ref[...] = jnp.zeros_like(m_ref)

    g = g_ref[...] + brow_ref[...]
    gt = gt_ref[...] + bcol_ref[...]
    li_col = _soft_cap(g)
    lf_col = _log_sigmoid(_soft_cap(g))
    li_row = _soft_cap(gt[0:H, :])
    lf_row = _log_sigmoid(_soft_cap(gt[H:2 * H, :]))

    row = lax.broadcasted_iota(jnp.int32, (L, L), 0)
    col = lax.broadcasted_iota(jnp.int32, (L, L), 1)
    causal = col <= row
    tri = causal.astype(F32)
    tri_t = (row <= col).astype(F32)
    b_col = _dot_f32(tri, lf_col)
    b_row = _dot_f32(lf_row, tri_t)
    a_col = li_col[:, 0:H] - b_col[:, H:2 * H]
    a_row = li_row - b_row

    scale = ML_DQK ** -0.5
    for h in range(H):
        qh = q_ref[:, h * ML_DQK:(h + 1) * ML_DQK]
        kh = k_ref[:, h * ML_DQK:(h + 1) * ML_DQK]
        vh = v_ref[:, h * ML_DV:(h + 1) * ML_DV]
        m_prev = m_ref[h:h + 1, 0:1]
        c_prev = c_ref[h]
        n_prev = n_ref[h:h + 1, :]

        amat = jnp.where(causal, a_row[h:h + 1, :], -jnp.inf)
        big_m = jnp.maximum(jnp.max(amat, axis=-1, keepdims=True), m_prev)
        dmat = jnp.exp(amat - big_m)
        inter = jnp.exp(m_prev - big_m)

        s = _dot_nt(qh, kh) * dmat
        qf = qh.astype(F32)
        num = _dot(s.astype(BF16), vh) + inter * _dot(qh, c_prev.astype(BF16))
        den = jnp.sum(s, axis=-1, keepdims=True) + inter * jnp.sum(qf * n_prev, axis=-1, keepdims=True)
        floor = jnp.exp(-(b_col[:, H + h:H + h + 1] + big_m))
        cell = (scale * num) / jnp.maximum(scale * jnp.abs(den), floor)

        w_head = nw_ref[:, h * ML_DV:(h + 1) * ML_DV]
        og = _sigmoid(o_ref[:, h * ML_DV:(h + 1) * ML_DV].astype(F32))
        out_ref[:, h * ML_DV:(h + 1) * ML_DV] = (_rms(cell, w_head) * og).astype(out_ref.dtype)

        m_last = big_m[L - 1:L, :]
        w_col = jnp.exp(a_col[:, h:h + 1] - m_last)
        decay = jnp.exp(m_prev - m_last)
        kw = kh.astype(F32) * w_col
        c_ref[h] = decay * c_prev + _dot_tn(kw.astype(BF16), vh)
        n_ref[h:h + 1, :] = decay * n_prev + jnp.sum(kw, axis=0, keepdims=True)
        m_new = b_col[L - 1:L, H + h:H + h + 1] + m_last
        m_ref[h:h + 1, :] = jnp.broadcast_to(m_new, (1, LANES))


def _mlstm(proj, gates, gates_t, bias_row, bias_col, norm_w, B, S):
    T = B * S
    L = min(ML_CHUNK, S)
    nc = S // L
    qk_w = ML_HEADS * ML_DQK
    v_w = ML_HEADS * ML_DV
    assert 2 * qk_w == v_w
    rows = lambda b, c: b * nc + c
    return pl.pallas_call(
        _mlstm_kernel,
        grid=(B, nc),
        in_specs=[pl.BlockSpec((L, qk_w), lambda b, c: (rows(b, c), 0)),
                  pl.BlockSpec((L, qk_w), lambda b, c: (rows(b, c), 1)),
                  pl.BlockSpec((L, v_w), lambda b, c: (rows(b, c), 1)),
                  pl.BlockSpec((L, v_w), lambda b, c: (rows(b, c), 2)),
                  pl.BlockSpec((L, LANES), lambda b, c: (rows(b, c), 0)),
                  pl.BlockSpec((2 * ML_HEADS, L), lambda b, c: (0, rows(b, c))),
                  pl.BlockSpec((1, LANES), lambda b, c: (0, 0)),
                  pl.BlockSpec((2 * ML_HEADS, 1), lambda b, c: (0, 0)),
                  pl.BlockSpec((1, v_w), lambda b, c: (0, 0))],
        out_specs=pl.BlockSpec((L, v_w), lambda b, c: (rows(b, c), 0)),
        out_shape=jax.ShapeDtypeStruct((T, v_w), BF16),
        scratch_shapes=[pltpu.VMEM((ML_HEADS, ML_DQK, ML_DV), F32),
                        pltpu.VMEM((ML_HEADS, ML_DQK), F32),
                        pltpu.VMEM((ML_HEADS, LANES), F32)],
        compiler_params=_cparams("arbitrary", "arbitrary"),
        name="mlstm",
    )(proj, proj, proj, proj, gates, gates_t, bias_row, bias_col, norm_w)


def _swa_kernel(sink_ref, q_ref, kp_ref, kc_ref, vp_ref, vc_ref, qw_ref, kw_ref, out_ref):
    n = pl.program_id(1)
    Q = q_ref.shape[0]
    qi = lax.broadcasted_iota(jnp.int32, (Q, 2 * Q), 0)
    kj = lax.broadcasted_iota(jnp.int32, (Q, 2 * Q), 1)
    dist = qi + Q - kj
    valid = (dist >= 0) & (dist < WINDOW) & ((kj >= Q) | (n > 0))
    dist_f = dist.astype(F32)
    qw = qw_ref[...] * (SW_HD ** -0.5)
    kw = kw_ref[...]

    outs = []
    for g in range(SW_KV_HEADS):
        cs = slice(g * SW_HD, (g + 1) * SW_HD)
        kf = jnp.concatenate([kp_ref[:, cs], kc_ref[:, cs]], axis=0).astype(F32)
        kn = _rms(kf, kw).astype(BF16)
        vg = jnp.concatenate([vp_ref[:, cs], vc_ref[:, cs]], axis=0)
        qs = []
        for hh in range(SW_GROUP):
            h = g * SW_GROUP + hh
            qf = q_ref[:, h * SW_HD:(h + 1) * SW_HD].astype(F32)
            qs.append(_rms(qf, qw).astype(BF16))
        sc_all = _dot_nt(jnp.concatenate(qs, axis=0), kn)
        ps = []
        for hh in range(SW_GROUP):
            h = g * SW_GROUP + hh
            slope = 2.0 ** (-8.0 * (h + 1) / SW_HEADS)
            sink = sink_ref[h]
            sc = sc_all[hh * Q:(hh + 1) * Q, :] - slope * dist_f
            sc = jnp.where(valid, sc, -jnp.inf)
            m = jnp.maximum(jnp.max(sc, axis=-1, keepdims=True), sink)
            e = jnp.exp(sc - m)
            denom = jnp.sum(e, axis=-1, keepdims=True) + jnp.exp(sink - m)
            ps.append((e / denom).astype(BF16))
        o_all = _dot(jnp.concatenate(ps, axis=0), vg)
        for hh in range(SW_GROUP):
            outs.append(o_all[hh * Q:(hh + 1) * Q, :])
    out_ref[...] = jnp.concatenate(outs, axis=-1).astype(out_ref.dtype)


def _swa(proj, sinks, q_norm_w, k_norm_w, B, S, q_off, k_off, v_off):
    T = B * S
    Q = WINDOW
    nb = S // Q
    q_w = SW_HEADS * SW_HD
    kv_w = SW_KV_HEADS * SW_HD
    cur = lambda b, n: b * nb + n
    prev = lambda b, n: b * nb + jnp.maximum(n - 1, 0)
    grid_spec = pltpu.PrefetchScalarGridSpec(
        num_scalar_prefetch=1,
        grid=(B, nb),
        in_specs=[pl.BlockSpec((Q, q_w), lambda b, n, s: (cur(b, n), q_off // q_w)),
                  pl.BlockSpec((Q, kv_w), lambda b, n, s: (prev(b, n), k_off // kv_w)),
                  pl.BlockSpec((Q, kv_w), lambda b, n, s: (cur(b, n), k_off // kv_w)),
                  pl.BlockSpec((Q, kv_w), lambda b, n, s: (prev(b, n), v_off // kv_w)),
                  pl.BlockSpec((Q, kv_w), lambda b, n, s: (cur(b, n), v_off // kv_w)),
                  pl.BlockSpec((1, SW_HD), lambda b, n, s: (0, 0)),
                  pl.BlockSpec((1, SW_HD), lambda b, n, s: (0, 0))],
        out_specs=pl.BlockSpec((Q, q_w), lambda b, n, s: (cur(b, n), 0)),
    )
    return pl.pallas_call(
        _swa_kernel,
        grid_spec=grid_spec,
        out_shape=jax.ShapeDtypeStruct((T, q_w), BF16),
        compiler_params=_cparams("arbitrary", "arbitrary"),
        name="swa",
    )(sinks, proj, proj, proj, proj, proj, q_norm_w, k_norm_w)


def _merge_kernel(a_ref, b_ref, wa_ref, wb_ref, ga_ref, gb_ref, o_ref):
    ya = _dot(a_ref[...], wa_ref[...])
    yb = _dot(b_ref[...], wb_ref[...])
    mixed = _sigmoid(ga_ref[...].astype(F32)) * ya + _sigmoid(gb_ref[...].astype(F32)) * yb
    o_ref[...] = mixed.astype(o_ref.dtype)


def _merge(ha, hb, wa, wb, proj, ga_off, gb_off):
    T, D = ha.shape
    N = wa.shape[1]
    tm = min(1024, T)
    tn = 1024
    return pl.pallas_call(
        _merge_kernel,
        grid=(T // tm, N // tn),
        in_specs=[pl.BlockSpec((tm, D), lambda i, j: (i, 0)),
                  pl.BlockSpec((tm, D), lambda i, j: (i, 0)),
                  pl.BlockSpec((D, tn), lambda i, j: (0, j)),
                  pl.BlockSpec((D, tn), lambda i, j: (0, j)),
                  pl.BlockSpec((tm, tn), lambda i, j: (i, ga_off // tn + j)),
                  pl.BlockSpec((tm, tn), lambda i, j: (i, gb_off // tn + j))],
        out_specs=pl.BlockSpec((tm, tn), lambda i, j: (i, j)),
        out_shape=jax.ShapeDtypeStruct((T, N), BF16),
        compiler_params=_cparams("arbitrary", "arbitrary"),
        name="merge",
    )(ha, hb, wa, wb, proj, proj)


def _out_proj_kernel(mix_ref, x_ref, wo_ref, nw_ref, wr_ref, br_ref, h_ref, xn_ref, idx_ref, gate_ref):
    h = x_ref[...] + _dot(mix_ref[...], wo_ref[...])
    h_ref[...] = h
    xn = _rms(h, nw_ref[...])
    xn_ref[...] = xn
    logits = _dot_f32(xn, wr_ref[...]) + br_ref[...]
    tm = logits.shape[0]
    lane_i = lax.broadcasted_iota(jnp.int32, (tm, LANES), 1)
    lane = lane_i.astype(F32)
    vals, idxs = [], []
    for _ in range(TOP_K):
        mx = jnp.max(logits, axis=-1, keepdims=True)
        ix = jnp.min(jnp.where(logits == mx, lane, float(LANES)), axis=-1, keepdims=True)
        vals.append(mx)
        idxs.append(ix)
        logits = jnp.where(lane == ix, -jnp.inf, logits)
    es = [jnp.exp(v - vals[0]) for v in vals]
    tot = es[0]
    for e in es[1:]:
        tot = tot + e
    idx_out = jnp.zeros((tm, LANES), F32)
    gate_out = jnp.zeros((tm, LANES), F32)
    for k in range(TOP_K):
        idx_out = jnp.where(lane_i == k, idxs[k], idx_out)
        gate_out = jnp.where(lane_i == k, es[k] / tot, gate_out)
    idx_ref[...] = idx_out.astype(jnp.int32)
    gate_ref[...] = gate_out


def _out_proj(mixed, x2d, w_out, norm_w, w_router, b_router):
    T, D = x2d.shape
    tm = min(512, T)
    row = lambda i: (i, 0)
    const = lambda i: (0, 0)
    return pl.pallas_call(
        _out_proj_kernel,
        grid=(T // tm,),
        in_specs=[pl.BlockSpec((tm, D), row), pl.BlockSpec((tm, D), row),
                  pl.BlockSpec((D, D), const), pl.BlockSpec((1, D), const),
                  pl.BlockSpec((D, LANES), const), pl.BlockSpec((1, LANES), const)],
        out_specs=[pl.BlockSpec((tm, D), row), pl.BlockSpec((tm, D), row),
                   pl.BlockSpec((tm, LANES), row), pl.BlockSpec((tm, LANES), row)],
        out_shape=[jax.ShapeDtypeStruct((T, D), F32), jax.ShapeDtypeStruct((T, D), F32),
                   jax.ShapeDtypeStruct((T, LANES), jnp.int32), jax.ShapeDtypeStruct((T, LANES), F32)],
        compiler_params=_cparams("arbitrary"),
        name="out_proj",
    )(mixed, x2d, w_out, norm_w, w_router, b_router)


def _gather_copy(src_hbm, dst_buf, sem, src_row, dst_row):
    return pltpu.make_async_copy(src_hbm.at[pl.ds(src_row, 1), :], dst_buf.at[pl.ds(dst_row, 1), :], sem)


def _expert_kernel(be_ref, nv_ref, tok_ref, tokn_ref, x_hbm, wg_ref, wl_ref, bg_ref, bl_ref, wd_ref, bd_ref,
                   o_ref, xbuf, xb, sem):
    i = pl.program_id(0)
    c = pl.program_id(1)
    n_valid = nv_ref[0]
    rows = xb.shape[0]
    slot = i % 2

    def issue(idx_ref, s):
        def body(r, carry):
            _gather_copy(x_hbm, xbuf.at[s], sem.at[s], idx_ref[0, 0, r], r).start()
            return carry
        lax.fori_loop(0, rows, body, 0, unroll=8)

    @pl.when((c == 0) & (i == 0))
    def _():
        issue(tok_ref, 0)

    @pl.when((c == 0) & (i + 1 < n_valid))
    def _():
        issue(tokn_ref, 1 - slot)

    @pl.when((c == 0) & (i < n_valid))
    def _():
        pltpu.make_async_copy(x_hbm.at[pl.ds(0, rows), :], xbuf.at[slot], sem.at[slot]).wait()
        xb[...] = xbuf[slot].astype(BF16)

    @pl.when(i < n_valid)
    def _():
        x = xb[...]
        h_glu = jnp.minimum(_dot(x, wg_ref[...]) + bg_ref[...], SWIGLU_LIMIT)
        h_lin = jnp.clip(_dot(x, wl_ref[...]) + bl_ref[...], -SWIGLU_LIMIT, SWIGLU_LIMIT)
        act = h_glu * _sigmoid(SWIGLU_ALPHA * h_glu) * (h_lin + 1.0)
        part = _dot(act.astype(BF16), wd_ref[...])

        @pl.when(c == 0)
        def _():
            o_ref[...] = part + bd_ref[...]

        @pl.when(c > 0)
        def _():
            o_ref[...] += part

    @pl.when((i >= n_valid) & (c == 0))
    def _():
        o_ref[...] = jnp.zeros_like(o_ref)


def _experts(xn, buf_tok, block_e, n_valid, w_up, b_up, w_down, b_down):
    T, D = xn.shape
    E, _, F2 = w_up.shape
    F = F2 // 2
    rows = MOE_ROWS
    fc = MOE_FF_CHUNK
    nfc = F // fc
    nblk = buf_tok.shape[0] // rows
    tok3 = buf_tok.reshape(nblk, 1, rows)
    ceff = lambda i, c, nv: jnp.where(i < nv[0], c, nfc - 1)
    grid_spec = pltpu.PrefetchScalarGridSpec(
        num_scalar_prefetch=2,
        grid=(nblk, nfc),
        in_specs=[
            pl.BlockSpec((1, 1, rows), lambda i, c, be, nv: (i, 0, 0), memory_space=pltpu.SMEM),
            pl.BlockSpec((1, 1, rows), lambda i, c, be, nv: (jnp.minimum(i + 1, nblk - 1), 0, 0),
                         memory_space=pltpu.SMEM),
            pl.BlockSpec(memory_space=pl.ANY),
            pl.BlockSpec((None, D, fc), lambda i, c, be, nv: (be[i], 0, ceff(i, c, nv))),
            pl.BlockSpec((None, D, fc), lambda i, c, be, nv: (be[i], 0, nfc + ceff(i, c, nv))),
            pl.BlockSpec((None, 1, fc), lambda i, c, be, nv: (be[i], 0, ceff(i, c, nv))),
            pl.BlockSpec((None, 1, fc), lambda i, c, be, nv: (be[i], 0, nfc + ceff(i, c, nv))),
            pl.BlockSpec((None, fc, D), lambda i, c, be, nv: (be[i], ceff(i, c, nv), 0)),
            pl.BlockSpec((None, 1, D), lambda i, c, be, nv: (be[i], 0, 0)),
        ],
        out_specs=pl.BlockSpec((rows, D), lambda i, c, be, nv: (i, 0)),
        scratch_shapes=[pltpu.VMEM((2, rows, D), F32), pltpu.VMEM((rows, D), BF16),
                        pltpu.SemaphoreType.DMA((2,))],
    )
    return pl.pallas_call(
        _expert_kernel,
        grid_spec=grid_spec,
        out_shape=jax.ShapeDtypeStruct((nblk * rows, D), F32),
        compiler_params=_cparams("arbitrary", "arbitrary"),
        name="experts",
    )(block_e, n_valid, tok3, tok3, xn, w_up, w_up, b_up, b_up, w_down, b_down)


def _combine_kernel(slot_ref, slotn_ref, y_hbm, h_ref, gate_ref, p_ref, nw_ref, wg_ref, wp_ref, o_ref, ybuf, sem):
    i = pl.program_id(0)
    nsteps = pl.num_programs(0)
    tm = h_ref.shape[0]
    n = TOP_K * tm
    s = i % 2

    def issue(idx_ref, b):
        def body(r, carry):
            _gather_copy(y_hbm, ybuf.at[b], sem.at[b], idx_ref[0, 0, r], r).start()
            return carry
        lax.fori_loop(0, n, body, 0, unroll=8)

    @pl.when(i == 0)
    def _():
        issue(slot_ref, 0)

    @pl.when(i + 1 < nsteps)
    def _():
        issue(slotn_ref, 1 - s)

    pltpu.make_async_copy(y_hbm.at[pl.ds(0, n), :], ybuf.at[s], sem.at[s]).wait()

    h = h_ref[...]
    gates = gate_ref[...]
    for k in range(TOP_K):
        h = h + gates[:, k:k + 1] * ybuf[s, k * tm:(k + 1) * tm, :]
    gate = _sigmoid(_dot(_rms(h, nw_ref[...]).astype(BF16), wg_ref[...]))
    o_ref[...] = h + gate * _dot(p_ref[...].astype(BF16), wp_ref[...])


def _combine(ys, slots, h1, gates, p2d, norm_w, w_gate, w_proj):
    T, D = h1.shape
    PD = p2d.shape[1]
    tm = min(256, T)
    nsteps = T // tm
    slot3 = slots.reshape(nsteps, tm, TOP_K).transpose(0, 2, 1).reshape(nsteps, 1, TOP_K * tm)
    row = lambda i: (i, 0)
    const = lambda i: (0, 0)
    return pl.pallas_call(
        _combine_kernel,
        grid=(nsteps,),
        in_specs=[pl.BlockSpec((1, 1, TOP_K * tm), lambda i: (i, 0, 0), memory_space=pltpu.SMEM),
                  pl.BlockSpec((1, 1, TOP_K * tm), lambda i: (jnp.minimum(i + 1, nsteps - 1), 0, 0),
                               memory_space=pltpu.SMEM),
                  pl.BlockSpec(memory_space=pl.ANY),
                  pl.BlockSpec((tm, D), row), pl.BlockSpec((tm, LANES), row), pl.BlockSpec((tm, PD), row),
                  pl.BlockSpec((1, D), const), pl.BlockSpec((D, D), const), pl.BlockSpec((PD, D), const)],
        out_specs=pl.BlockSpec((tm, D), row),
        out_shape=jax.ShapeDtypeStruct((T, D), F32),
        scratch_shapes=[pltpu.VMEM((2, TOP_K * tm, D), F32), pltpu.SemaphoreType.DMA((2,))],
        compiler_params=_cparams("arbitrary"),
        name="combine",
    )(slot3, slot3, ys, h1, gates, p2d, norm_w, w_gate, w_proj)


def _routing(top_idx):
    T, K = top_idx.shape
    N = T * K
    E = N_EXPERTS
    rows = MOE_ROWS
    flat_e = top_idx.reshape(-1)
    order = jnp.argsort(flat_e, stable=True).astype(jnp.int32)
    e_sorted = flat_e[order]
    counts = jnp.bincount(flat_e, length=E).astype(jnp.int32)
    starts = jnp.cumsum(counts) - counts
    padded = (counts + rows - 1) // rows * rows
    pad_ends = jnp.cumsum(padded)
    pad_starts = pad_ends - padded
    slot_sorted = pad_starts[e_sorted] + jnp.arange(N, dtype=jnp.int32) - starts[e_sorted]
    slots = jnp.zeros((N,), jnp.int32).at[order].set(slot_sorted, unique_indices=True)
    P = N + E * rows
    nblk = P // rows
    buf_tok = jnp.zeros((P,), jnp.int32).at[slot_sorted].set(order // K, unique_indices=True)
    block_e = jnp.minimum(jnp.searchsorted(pad_ends, jnp.arange(nblk, dtype=jnp.int32) * rows, side='right'),
                          E - 1).astype(jnp.int32)
    n_valid = (pad_ends[-1] // rows).astype(jnp.int32).reshape(1)
    return slots.reshape(T, K), buf_tok, block_e, n_valid


def _layer(h2d, p2d, B, S, norm_mix_w, w_in, ml_igate_b, ml_fgate_b, ml_norm_w, sw_q_norm_w, sw_k_norm_w, sw_sinks,
           w_branch_a, w_branch_b, w_out, norm_ffn_w, w_router, b_router, w_expert_up, b_expert_up,
           w_expert_down, b_expert_down, norm_ple_w, w_ple_gate, w_ple_proj):
    T, D = h2d.shape
    H = ML_HEADS
    qk_w, v_w = H * ML_DQK, H * ML_DV
    sq_w, skv_w = SW_HEADS * SW_HD, SW_KV_HEADS * SW_HD
    g0 = 2 * qk_w + 2 * v_w
    s0 = g0 + 2 * H
    w_sq = w_in[:, s0:s0 + sq_w]
    w_skv = w_in[:, s0 + sq_w:s0 + sq_w + 2 * skv_w]
    w_gab = w_in[:, s0 + sq_w + 2 * skv_w:]
    w_main = jnp.concatenate([w_in[:, :g0], w_sq, w_gab, w_skv], axis=1).astype(BF16)
    w_gate = jnp.pad(w_in[:, g0:s0], ((0, 0), (0, LANES - 2 * H))).astype(BF16)
    sq_off = g0
    ga_off = sq_off + sq_w
    gb_off = ga_off + D
    sk_off = gb_off + D
    sv_off = sk_off + skv_w

    proj, gates = _in_proj(h2d, norm_mix_w.reshape(1, D), w_main, w_gate)

    gate_b = jnp.concatenate([ml_igate_b, ml_fgate_b]).astype(F32)
    bias_row = jnp.pad(gate_b, (0, LANES - 2 * H)).reshape(1, LANES)
    bias_col = gate_b.reshape(2 * H, 1)
    gates_t = gates[:, :2 * H].T
    h_ml = _mlstm(proj, gates, gates_t, bias_row, bias_col, ml_norm_w.reshape(1, v_w), B, S)

    h_sw = _swa(proj, sw_sinks.astype(F32), sw_q_norm_w.reshape(1, SW_HD), sw_k_norm_w.reshape(1, SW_HD),
                B, S, sq_off, sk_off, sv_off)

    mixed = _merge(h_ml, h_sw, w_branch_a.astype(BF16), w_branch_b.astype(BF16), proj, ga_off, gb_off)

    wr = jnp.pad(w_router, ((0, 0), (0, LANES - N_EXPERTS)))
    br = jnp.pad(b_router, (0, LANES - N_EXPERTS), constant_values=NEG_BIG).reshape(1, LANES)
    h1, xn, idx_pad, gate_pad = _out_proj(mixed, h2d, w_out.astype(BF16), norm_ffn_w.reshape(1, D), wr, br)

    slots, buf_tok, block_e, n_valid = _routing(idx_pad[:, :TOP_K])
    E = N_EXPERTS
    ys = _experts(xn, buf_tok, block_e, n_valid,
                  w_expert_up.astype(BF16), b_expert_up.reshape(E, 1, -1),
                  w_expert_down.astype(BF16), b_expert_down.reshape(E, 1, -1))

    return _combine(ys, slots, h1, gate_pad, p2d, norm_ple_w.reshape(1, D),
                    w_ple_gate.astype(BF16), w_ple_proj.astype(BF16))


def kernel(x, p, norm_mix_w, w_in, ml_igate_b, ml_fgate_b, ml_norm_w, sw_q_norm_w, sw_k_norm_w, sw_sinks,
           w_branch_a, w_branch_b, w_out, norm_ffn_w, w_router, b_router, w_expert_up, b_expert_up,
           w_expert_down, b_expert_down, norm_ple_w, w_ple_gate, w_ple_proj):
    B, S, D = x.shape
    depth = p.shape[0]
    h = x.reshape(B * S, D)
    for i in range(depth):
        h = _layer(h, p[i].reshape(B * S, -1), B, S, norm_mix_w[i], w_in[i], ml_igate_b[i], ml_fgate_b[i],
                   ml_norm_w[i], sw_q_norm_w[i], sw_k_norm_w[i], sw_sinks[i], w_branch_a[i], w_branch_b[i],
                   w_out[i], norm_ffn_w[i], w_router[i], b_router[i], w_expert_up[i], b_expert_up[i],
                   w_expert_down[i], b_expert_down[i], norm_ple_w[i], w_ple_gate[i], w_ple_proj[i])
    return h.reshape(B, S, D)
```

```python
import functools

import jax
import jax.numpy as jnp
from jax import lax
from jax.experimental import pallas as pl
from jax.experimental.pallas import tpu as pltpu

F32 = jnp.float32
BF16 = jnp.bfloat16

EPS = 1e-6
ML_HEADS = 8
ML_DQK = 128
ML_DV = 256
GATE_CAP = 15.0
SW_HEADS = 32
SW_KV_HEADS = 4
SW_GROUP = SW_HEADS // SW_KV_HEADS
SW_HD = 64
WINDOW = 128
N_EXPERTS = 32
TOP_K = 4
SWIGLU_LIMIT = 7.0
SWIGLU_ALPHA = 1.702

LANES = 128
VMEM_LIMIT = 56 * 1024 * 1024

ML_CHUNK = 256
MOE_ROWS = 512
MOE_FF_CHUNK = 1024
MOE_OUT_CHUNK = 1024
ROUTE_ROWS = 512
NEG_BIG = -1e30


def _cparams(*sem):
    return pltpu.CompilerParams(dimension_semantics=sem, vmem_limit_bytes=VMEM_LIMIT)


def _dot(a, b):
    return jnp.dot(a, b, preferred_element_type=F32)


def _dot_nt(a, b):
    return lax.dot_general(a, b, (((1,), (1,)), ((), ())), preferred_element_type=F32)


def _dot_tn(a, b):
    return lax.dot_general(a, b, (((0,), (0,)), ((), ())), preferred_element_type=F32)


def _dot_f32(a, b):
    return jnp.dot(a, b, preferred_element_type=F32, precision=lax.Precision.HIGHEST)


def _rms(x, w):
    return x * lax.rsqrt(jnp.mean(x * x, axis=-1, keepdims=True) + EPS) * w


def _sigmoid(x):
    return 1.0 / (1.0 + jnp.exp(-x))


def _in_proj_kernel(x_ref, nw_ref, w_ref, wg_ref, o_ref, g_ref, xn_ref):
    @pl.when(pl.program_id(1) == 0)
    def _():
        xn = _rms(x_ref[...], nw_ref[...]).astype(BF16)
        xn_ref[...] = xn
        g_ref[...] = _dot(xn, wg_ref[...])

    o_ref[...] = _dot(xn_ref[...], w_ref[...]).astype(o_ref.dtype)


def _in_proj(x2d, norm_w, w_main, w_gate):
    T, D = x2d.shape
    N = w_main.shape[1]
    tm = min(1024, T)
    tn = 1280
    return pl.pallas_call(
        _in_proj_kernel,
        grid=(T // tm, N // tn),
        in_specs=[pl.BlockSpec((tm, D), lambda i, j: (i, 0)),
                  pl.BlockSpec((1, D), lambda i, j: (0, 0)),
                  pl.BlockSpec((D, tn), lambda i, j: (0, j)),
                  pl.BlockSpec((D, LANES), lambda i, j: (0, 0))],
        out_specs=[pl.BlockSpec((tm, tn), lambda i, j: (i, j)),
                   pl.BlockSpec((tm, LANES), lambda i, j: (i, 0))],
        out_shape=[jax.ShapeDtypeStruct((T, N), BF16), jax.ShapeDtypeStruct((T, LANES), F32)],
        scratch_shapes=[pltpu.VMEM((tm, D), BF16)],
        compiler_params=_cparams("arbitrary", "arbitrary"),
        name="in_proj",
    )(x2d, norm_w, w_main, w_gate)


def _log_sigmoid(z):
    return jnp.minimum(z, 0.0) - jnp.log1p(jnp.exp(-jnp.abs(z)))


def _soft_cap(z):
    return GATE_CAP * jnp.tanh(z / GATE_CAP)


def _mlstm_kernel(q_ref, k_ref, v_ref, o_ref, g_ref, gt_ref, brow_ref, bcol_ref, nw_ref,
                  out_ref, c_ref, n_ref, m_ref):
    L = q_ref.shape[0]
    H = ML_HEADS

    @pl.when(pl.program_id(1) == 0)
    def _():
        c_ref[...] = jnp.zeros_like(c_ref)
        n_ref[...] = jnp.zeros_like(n_ref)
        m_ref[...] = jnp.zeros_like(m_ref)

    g = g_ref[...] + brow_ref[...]
    gt = gt_ref[...] + bcol_ref[...]
    li_col = _soft_cap(g)
    lf_col = _log_sigmoid(_soft_cap(g))
    li_row = _soft_cap(gt[0:H, :])
    lf_row = _log_sigmoid(_soft_cap(gt[H:2 * H, :]))

    row = lax.broadcasted_iota(jnp.int32, (L, L), 0)
    col = lax.broadcasted_iota(jnp.int32, (L, L), 1)
    causal = col <= row
    tri = causal.astype(F32)
    tri_t = (row <= col).astype(F32)
    b_col = _dot_f32(tri, lf_col)
    b_row = _dot_f32(lf_row, tri_t)
    a_col = li_col[:, 0:H] - b_col[:, H:2 * H]
    a_row = li_row - b_row

    scale = ML_DQK ** -0.5
    for h in range(H):
        qh = q_ref[:, h * ML_DQK:(h + 1) * ML_DQK]
        kh = k_ref[:, h * ML_DQK:(h + 1) * ML_DQK]
        vh = v_ref[:, h * ML_DV:(h + 1) * ML_DV]
        m_prev = m_ref[h:h + 1, 0:1]
        c_prev = c_ref[h]
        n_prev = n_ref[h:h + 1, :]

        amat = jnp.where(causal, a_row[h:h + 1, :], -jnp.inf)
        big_m = jnp.maximum(jnp.max(amat, axis=-1, keepdims=True), m_prev)
        dmat = jnp.exp(amat - big_m)
        inter = jnp.exp(m_prev - big_m)

        s = _dot_nt(qh, kh) * dmat
        qf = qh.astype(F32)
        num = _dot(s.astype(BF16), vh) + inter * _dot(qh, c_prev.astype(BF16))
        den = jnp.sum(s, axis=-1, keepdims=True) + inter * jnp.sum(qf * n_prev, axis=-1, keepdims=True)
        floor = jnp.exp(-(b_col[:, H + h:H + h + 1] + big_m))
        cell = (scale * num) / jnp.maximum(scale * jnp.abs(den), floor)

        w_head = nw_ref[:, h * ML_DV:(h + 1) * ML_DV]
        og = _sigmoid(o_ref[:, h * ML_DV:(h + 1) * ML_DV].astype(F32))
        out_ref[:, h * ML_DV:(h + 1) * ML_DV] = (_rms(cell, w_head) * og).astype(out_ref.dtype)

        m_last = big_m[L - 1:L, :]
        w_col = jnp.exp(a_col[:, h:h + 1] - m_last)
        decay = jnp.exp(m_prev - m_last)
        kw = kh.astype(F32) * w_col
        c_ref[h] = decay * c_prev + _dot_tn(kw.astype(BF16), vh)
        n_ref[h:h + 1, :] = decay * n_prev + jnp.sum(kw, axis=0, keepdims=True)
        m_new = b_col[L - 1:L, H + h:H + h + 1] + m_last
        m_ref[h:h + 1, :] = jnp.broadcast_to(m_new, (1, LANES))


def _mlstm(proj, gates, gates_t, bias_row, bias_col, norm_w, B, S):
    T = B * S
    L = min(ML_CHUNK, S)
    nc = S // L
    qk_w = ML_HEADS * ML_DQK
    v_w = ML_HEADS * ML_DV
    assert 2 * qk_w == v_w
    rows = lambda b, c: b * nc + c
    return pl.pallas_call(
        _mlstm_kernel,
        grid=(B, nc),
        in_specs=[pl.BlockSpec((L, qk_w), lambda b, c: (rows(b, c), 0)),
                  pl.BlockSpec((L, qk_w), lambda b, c: (rows(b, c), 1)),
                  pl.BlockSpec((L, v_w), lambda b, c: (rows(b, c), 1)),
                  pl.BlockSpec((L, v_w), lambda b, c: (rows(b, c), 2)),
                  pl.BlockSpec((L, LANES), lambda b, c: (rows(b, c), 0)),
                  pl.BlockSpec((2 * ML_HEADS, L), lambda b, c: (0, rows(b, c))),
                  pl.BlockSpec((1, LANES), lambda b, c: (0, 0)),
                  pl.BlockSpec((2 * ML_HEADS, 1), lambda b, c: (0, 0)),
                  pl.BlockSpec((1, v_w), lambda b, c: (0, 0))],
        out_specs=pl.BlockSpec((L, v_w), lambda b, c: (rows(b, c), 0)),
        out_shape=jax.ShapeDtypeStruct((T, v_w), BF16),
        scratch_shapes=[pltpu.VMEM((ML_HEADS, ML_DQK, ML_DV), F32),
                        pltpu.VMEM((ML_HEADS, ML_DQK), F32),
                        pltpu.VMEM((ML_HEADS, LANES), F32)],
        compiler_params=_cparams("arbitrary", "arbitrary"),
        name="mlstm",
    )(proj, proj, proj, proj, gates, gates_t, bias_row, bias_col, norm_w)


def _swa_kernel(sink_ref, q_ref, kp_ref, kc_ref, vp_ref, vc_ref, qw_ref, kw_ref, out_ref):
    n = pl.program_id(1)
    Q = q_ref.shape[0]
    qi = lax.broadcasted_iota(jnp.int32, (Q, 2 * Q), 0)
    kj = lax.broadcasted_iota(jnp.int32, (Q, 2 * Q), 1)
    dist = qi + Q - kj
    valid = (dist >= 0) & (dist < WINDOW) & ((kj >= Q) | (n > 0))
    dist_f = dist.astype(F32)
    HD = SW_HD
    assert LANES == 2 * HD

    gr = lax.broadcasted_iota(jnp.int32, (LANES, LANES), 0) // HD
    gc = lax.broadcasted_iota(jnp.int32, (LANES, LANES), 1) // HD
    gmat = (gr == gc).astype(BF16)

    def group_rms_chunks(x, gain):
        rows, width = x.shape
        nchunk = width // LANES
        x2 = x * x
        p0 = x2.astype(BF16)
        r1 = x2 - p0.astype(F32)
        p1 = r1.astype(BF16)
        p2 = (r1 - p1.astype(F32)).astype(BF16)
        pieces = [p[:, c * LANES:(c + 1) * LANES] for p in (p0, p1, p2) for c in range(nchunk)]
        sums = _dot(jnp.concatenate(pieces, axis=0), gmat)
        out = []
        for c in range(nchunk):
            ss = (sums[c * rows:(c + 1) * rows] + sums[(nchunk + c) * rows:(nchunk + c + 1) * rows]
                  + sums[(2 * nchunk + c) * rows:(2 * nchunk + c + 1) * rows])
            sl = slice(c * LANES, (c + 1) * LANES)
            out.append(x[:, sl] * lax.rsqrt(ss * (1.0 / HD) + EPS) * gain[:, sl])
        return out

    qn = [c.astype(BF16) for c in group_rms_chunks(q_ref[...].astype(F32), qw_ref[...] * (HD ** -0.5))]
    kn = group_rms_chunks(jnp.concatenate([kp_ref[...], kc_ref[...]], axis=0).astype(F32), kw_ref[...])
    vf = jnp.concatenate([vp_ref[...], vc_ref[...]], axis=0).astype(F32)
    lo_half = lax.broadcasted_iota(jnp.int32, (2 * Q, LANES), 1) < HD

    for g in range(SW_KV_HEADS):
        kc = kn[g // 2]
        vc = vf[:, (g // 2) * LANES:(g // 2 + 1) * LANES]
        kr = pltpu.roll(kc, HD, axis=1)
        vr = pltpu.roll(vc, HD, axis=1)
        k_lo, k_hi, v_lo, v_hi = (kc, kr, vc, vr) if g % 2 == 0 else (kr, kc, vr, vc)
        ka = jnp.where(lo_half, k_lo, 0.0).astype(BF16)
        kb = jnp.where(lo_half, 0.0, k_hi).astype(BF16)
        vv = jnp.concatenate([jnp.where(lo_half, v_lo, 0.0), jnp.where(lo_half, 0.0, v_hi)], axis=0).astype(BF16)
        nch = SW_GROUP // 2
        lhs = jnp.concatenate(qn[g * nch:(g + 1) * nch], axis=0)
        scores = (_dot_nt(lhs, ka), _dot_nt(lhs, kb))
        probs = ([], [])
        for m in range(nch):
            for par in range(2):
                h = g * SW_GROUP + 2 * m + par
                slope = 2.0 ** (-8.0 * (h + 1) / SW_HEADS)
                sink = sink_ref[h]
                sc = scores[par][m * Q:(m + 1) * Q, :] - slope * dist_f
                sc = jnp.where(valid, sc, -jnp.inf)
                mx = jnp.maximum(jnp.max(sc, axis=-1, keepdims=True), sink)
                e = jnp.exp(sc - mx)
                denom = jnp.sum(e, axis=-1, keepdims=True) + jnp.exp(sink - mx)
                probs[par].append((e / denom).astype(BF16))
        p_all = jnp.concatenate([jnp.concatenate(probs[0], axis=0), jnp.concatenate(probs[1], axis=0)], axis=-1)
        o_all = _dot(p_all, vv)
        for m in range(nch):
            c0 = (g * nch + m) * LANES
            out_ref[:, c0:c0 + LANES] = o_all[m * Q:(m + 1) * Q, :].astype(out_ref.dtype)


def _swa(proj, sinks, q_norm_w, k_norm_w, B, S, q_off, k_off, v_off):
    T = B * S
    Q = WINDOW
    nb = S // Q
    q_w = SW_HEADS * SW_HD
    kv_w = SW_KV_HEADS * SW_HD
    cur = lambda b, n: b * nb + n
    prev = lambda b, n: b * nb + jnp.maximum(n - 1, 0)
    grid_spec = pltpu.PrefetchScalarGridSpec(
        num_scalar_prefetch=1,
        grid=(B, nb),
        in_specs=[pl.BlockSpec((Q, q_w), lambda b, n, s: (cur(b, n), q_off // q_w)),
                  pl.BlockSpec((Q, kv_w), lambda b, n, s: (prev(b, n), k_off // kv_w)),
                  pl.BlockSpec((Q, kv_w), lambda b, n, s: (cur(b, n), k_off // kv_w)),
                  pl.BlockSpec((Q, kv_w), lambda b, n, s: (prev(b, n), v_off // kv_w)),
                  pl.BlockSpec((Q, kv_w), lambda b, n, s: (cur(b, n), v_off // kv_w)),
                  pl.BlockSpec((1, q_w), lambda b, n, s: (0, 0)),
                  pl.BlockSpec((1, kv_w), lambda b, n, s: (0, 0))],
        out_specs=pl.BlockSpec((Q, q_w), lambda b, n, s: (cur(b, n), 0)),
    )
    q_gain = jnp.tile(q_norm_w.reshape(1, SW_HD), (1, SW_HEADS))
    k_gain = jnp.tile(k_norm_w.reshape(1, SW_HD), (1, SW_KV_HEADS))
    return pl.pallas_call(
        _swa_kernel,
        grid_spec=grid_spec,
        out_shape=jax.ShapeDtypeStruct((T, q_w), BF16),
        compiler_params=_cparams("arbitrary", "arbitrary"),
        name="swa",
    )(sinks, proj, proj, proj, proj, proj, q_gain, k_gain)


def _merge_kernel(a_ref, b_ref, wa_ref, wb_ref, ga_ref, gb_ref, o_ref):
    ya = _dot(a_ref[...], wa_ref[...])
    yb = _dot(b_ref[...], wb_ref[...])
    mixed = _sigmoid(ga_ref[...].astype(F32)) * ya + _sigmoid(gb_ref[...].astype(F32)) * yb
    o_ref[...] = mixed.astype(o_ref.dtype)


def _merge(ha, hb, wa, wb, proj, ga_off, gb_off):
    T, D = ha.shape
    N = wa.shape[1]
    tm = min(1024, T)
    tn = 1024
    return pl.pallas_call(
        _merge_kernel,
        grid=(T // tm, N // tn),
        in_specs=[pl.BlockSpec((tm, D), lambda i, j: (i, 0)),
                  pl.BlockSpec((tm, D), lambda i, j: (i, 0)),
                  pl.BlockSpec((D, tn), lambda i, j: (0, j)),
                  pl.BlockSpec((D, tn), lambda i, j: (0, j)),
                  pl.BlockSpec((tm, tn), lambda i, j: (i, ga_off // tn + j)),
                  pl.BlockSpec((tm, tn), lambda i, j: (i, gb_off // tn + j))],
        out_specs=pl.BlockSpec((tm, tn), lambda i, j: (i, j)),
        out_shape=jax.ShapeDtypeStruct((T, N), BF16),
        compiler_params=_cparams("arbitrary", "arbitrary"),
        name="merge",
    )(ha, hb, wa, wb, proj, proj)


def _out_proj_kernel(mix_ref, x_ref, wo_ref, nw_ref, wr_ref, br_ref, h_ref, xn_ref, idx_ref, gate_ref):
    h = x_ref[...] + _dot(mix_ref[...], wo_ref[...])
    h_ref[...] = h
    xn = _rms(h, nw_ref[...])
    xn_ref[...] = xn
    logits = _dot_f32(xn, wr_ref[...]) + br_ref[...]
    tm = logits.shape[0]
    lane_i = lax.broadcasted_iota(jnp.int32, (tm, LANES), 1)
    lane = lane_i.astype(F32)
    vals, idxs = [], []
    for _ in range(TOP_K):
        mx = jnp.max(logits, axis=-1, keepdims=True)
        ix = jnp.min(jnp.where(logits == mx, lane, float(LANES)), axis=-1, keepdims=True)
        vals.append(mx)
        idxs.append(ix)
        logits = jnp.where(lane == ix, -jnp.inf, logits)
    es = [jnp.exp(v - vals[0]) for v in vals]
    tot = es[0]
    for e in es[1:]:
        tot = tot + e
    idx_out = jnp.zeros((tm, LANES), F32)
    gate_out = jnp.zeros((tm, LANES), F32)
    for k in range(TOP_K):
        idx_out = jnp.where(lane_i == k, idxs[k], idx_out)
        gate_out = jnp.where(lane_i == k, es[k] / tot, gate_out)
    idx_ref[...] = idx_out.astype(jnp.int32)
    gate_ref[...] = gate_out


def _out_proj(mixed, x2d, w_out, norm_w, w_router, b_router):
    T, D = x2d.shape
    tm = min(512, T)
    row = lambda i: (i, 0)
    const = lambda i: (0, 0)
    return pl.pallas_call(
        _out_proj_kernel,
        grid=(T // tm,),
        in_specs=[pl.BlockSpec((tm, D), row), pl.BlockSpec((tm, D), row),
                  pl.BlockSpec((D, D), const), pl.BlockSpec((1, D), const),
                  pl.BlockSpec((D, LANES), const), pl.BlockSpec((1, LANES), const)],
        out_specs=[pl.BlockSpec((tm, D), row), pl.BlockSpec((tm, D), row),
                   pl.BlockSpec((tm, LANES), row), pl.BlockSpec((tm, LANES), row)],
        out_shape=[jax.ShapeDtypeStruct((T, D), F32), jax.ShapeDtypeStruct((T, D), F32),
                   jax.ShapeDtypeStruct((T, LANES), jnp.int32), jax.ShapeDtypeStruct((T, LANES), F32)],
        compiler_params=_cparams("arbitrary"),
        name="out_proj",
    )(mixed, x2d, w_out, norm_w, w_router, b_router)


def _gather_copy(src_hbm, dst_buf, sem, src_row, dst_row):
    return pltpu.make_async_copy(src_hbm.at[pl.ds(src_row, 1), :], dst_buf.at[pl.ds(dst_row, 1), :], sem)


def _expert_kernel(be_ref, nv_ref, tok_ref, tokn_ref, x_hbm, wg_ref, wl_ref, bg_ref, bl_ref, wd_ref, bd_ref,
                   o_ref, xbuf, xb, act, sem):
    i = pl.program_id(0)
    c = pl.program_id(1)
    n_valid = nv_ref[0]
    rows = xb.shape[0]
    nu = act.shape[0]

    @pl.when((c == 0) & (i == 0))
    def _():
        def body(r, carry):
            _gather_copy(x_hbm, xbuf, sem, tok_ref[0, 0, r], r).start()
            return carry
        lax.fori_loop(0, rows, body, 0, unroll=8)

    @pl.when((c == 0) & (i <= n_valid))
    def _():
        pltpu.make_async_copy(x_hbm.at[pl.ds(0, rows), :], xbuf, sem).wait()

    @pl.when((c == 0) & (i < n_valid))
    def _():
        xb[...] = xbuf[...].astype(BF16)

    def request_next_rows():
        per = rows // (nu + xb.shape[1] // o_ref.shape[1])
        dst = xbuf.at[pl.ds(pl.multiple_of(c * per, per), per), :]
        for r in range(per):
            _gather_copy(x_hbm, dst, sem, tokn_ref[0, 0, c * per + r], r).start()

    @pl.when((i < n_valid) & (c < nu))
    def _():
        request_next_rows()
        x = xb[...]
        h_glu = jnp.minimum(_dot(x, wg_ref[...]) + bg_ref[...], SWIGLU_LIMIT)
        h_lin = jnp.clip(_dot(x, wl_ref[...]) + bl_ref[...], -SWIGLU_LIMIT, SWIGLU_LIMIT)
        act[c] = (h_glu * _sigmoid(SWIGLU_ALPHA * h_glu) * (h_lin + 1.0)).astype(BF16)

    @pl.when((i < n_valid) & (c >= nu))
    def _():
        request_next_rows()
        fc = act.shape[2]
        y = _dot(act[0], wd_ref[0:fc, :])
        for u in range(1, nu):
            y = y + _dot(act[u], wd_ref[u * fc:(u + 1) * fc, :])
        o_ref[...] = y + bd_ref[...]

    @pl.when((i >= n_valid) & (c >= nu))
    def _():
        o_ref[...] = jnp.zeros_like(o_ref)


def _experts(xn, buf_tok, block_e, n_valid, w_up, b_up, w_down, b_down):
    T, D = xn.shape
    E, _, F2 = w_up.shape
    F = F2 // 2
    rows = MOE_ROWS
    fc = MOE_FF_CHUNK
    dn = MOE_OUT_CHUNK
    nu = F // fc
    nd = D // dn
    nblk = buf_tok.shape[0] // rows
    tok3 = buf_tok.reshape(nblk, 1, rows)
    cu = lambda i, c, nv: jnp.where(i < nv[0], jnp.minimum(c, nu - 1), nu - 1)
    cd = lambda i, c, nv: jnp.where(i < nv[0], jnp.maximum(c - nu, 0), nd - 1)
    co = lambda c: jnp.maximum(c - nu, 0)
    grid_spec = pltpu.PrefetchScalarGridSpec(
        num_scalar_prefetch=2,
        grid=(nblk, nu + nd),
        in_specs=[
            pl.BlockSpec((1, 1, rows), lambda i, c, be, nv: (i, 0, 0), memory_space=pltpu.SMEM),
            pl.BlockSpec((1, 1, rows), lambda i, c, be, nv: (jnp.minimum(i + 1, nblk - 1), 0, 0),
                         memory_space=pltpu.SMEM),
            pl.BlockSpec(memory_space=pl.ANY),
            pl.BlockSpec((None, D, fc), lambda i, c, be, nv: (be[i], 0, cu(i, c, nv))),
            pl.BlockSpec((None, D, fc), lambda i, c, be, nv: (be[i], 0, nu + cu(i, c, nv))),
            pl.BlockSpec((None, 1, fc), lambda i, c, be, nv: (be[i], 0, cu(i, c, nv))),
            pl.BlockSpec((None, 1, fc), lambda i, c, be, nv: (be[i], 0, nu + cu(i, c, nv))),
            pl.BlockSpec((None, F, dn), lambda i, c, be, nv: (be[i], 0, cd(i, c, nv))),
            pl.BlockSpec((None, 1, dn), lambda i, c, be, nv: (be[i], 0, cd(i, c, nv))),
        ],
        out_specs=pl.BlockSpec((rows, dn), lambda i, c, be, nv: (i, co(c))),
        scratch_shapes=[pltpu.VMEM((rows, D), F32), pltpu.VMEM((rows, D), BF16),
                        pltpu.VMEM((nu, rows, fc), BF16), pltpu.SemaphoreType.DMA],
    )
    return pl.pallas_call(
        _expert_kernel,
        grid_spec=grid_spec,
        out_shape=jax.ShapeDtypeStruct((nblk * rows, D), F32),
        compiler_params=_cparams("arbitrary", "arbitrary"),
        name="experts",
    )(block_e, n_valid, tok3, tok3, xn, w_up, w_up, b_up, b_up, w_down, b_down)


def _combine_kernel(slot_ref, slotn_ref, y_hbm, h_ref, gate_ref, p_ref, nw_ref, wg_ref, wp_ref, o_ref, ybuf, sem):
    i = pl.program_id(0)
    nsteps = pl.num_programs(0)
    tm = h_ref.shape[0]
    n = TOP_K * tm
    s = i % 2

    def wait_rows(b):
        pltpu.make_async_copy(y_hbm.at[pl.ds(0, n), :], ybuf.at[b], sem.at[b]).wait()

    @pl.when(i == 0)
    def _():
        def body(r, carry):
            _gather_copy(y_hbm, ybuf.at[0], sem.at[0], slot_ref[0, 0, r], r).start()
            return carry
        lax.fori_loop(0, n, body, 0, unroll=8)

    wait_rows(s)
    nxt = ybuf.at[1 - s]
    for r in range(n):
        _gather_copy(y_hbm, nxt, sem.at[1 - s], slotn_ref[0, 0, r], r).start()

    h = h_ref[...]
    gates = gate_ref[...]
    for k in range(TOP_K):
        h = h + gates[:, k:k + 1] * ybuf[s, k * tm:(k + 1) * tm, :]
    gate = _sigmoid(_dot(_rms(h, nw_ref[...]).astype(BF16), wg_ref[...]))
    o_ref[...] = h + gate * _dot(p_ref[...].astype(BF16), wp_ref[...])

    @pl.when(i == nsteps - 1)
    def _():
        wait_rows(1 - s)


def _combine(ys, slots, h1, gates, p2d, norm_w, w_gate, w_proj):
    T, D = h1.shape
    PD = p2d.shape[1]
    tm = min(256, T)
    nsteps = T // tm
    slot3 = slots.reshape(nsteps, tm, TOP_K).transpose(0, 2, 1).reshape(nsteps, 1, TOP_K * tm)
    row = lambda i: (i, 0)
    const = lambda i: (0, 0)
    return pl.pallas_call(
        _combine_kernel,
        grid=(nsteps,),
        in_specs=[pl.BlockSpec((1, 1, TOP_K * tm), lambda i: (i, 0, 0), memory_space=pltpu.SMEM),
                  pl.BlockSpec((1, 1, TOP_K * tm), lambda i: (jnp.minimum(i + 1, nsteps - 1), 0, 0),
                               memory_space=pltpu.SMEM),
                  pl.BlockSpec(memory_space=pl.ANY),
                  pl.BlockSpec((tm, D), row), pl.BlockSpec((tm, LANES), row), pl.BlockSpec((tm, PD), row),
                  pl.BlockSpec((1, D), const), pl.BlockSpec((D, D), const), pl.BlockSpec((PD, D), const)],
        out_specs=pl.BlockSpec((tm, D), row),
        out_shape=jax.ShapeDtypeStruct((T, D), F32),
        scratch_shapes=[pltpu.VMEM((2, TOP_K * tm, D), F32), pltpu.SemaphoreType.DMA((2,))],
        compiler_params=_cparams("arbitrary"),
        name="combine",
    )(slot3, slot3, ys, h1, gates, p2d, norm_w, w_gate, w_proj)


def _route_rank_kernel(idx_ref, rank_ref, cnt_ref, base_ref):
    @pl.when(pl.program_id(0) == 0)
    def _():
        base_ref[...] = jnp.zeros_like(base_ref)

    tm = idx_ref.shape[0]
    lane = lax.broadcasted_iota(jnp.int32, (tm, LANES), 1)
    idx = idx_ref[...]
    onehots = [(lane == idx[:, k:k + 1]).astype(F32) for k in range(TOP_K)]
    anyhot = onehots[0]
    for oh in onehots[1:]:
        anyhot = anyhot + oh
    r = lax.broadcasted_iota(jnp.int32, (tm, tm), 0)
    c = lax.broadcasted_iota(jnp.int32, (tm, tm), 1)
    before = (c < r).astype(BF16)
    prior = _dot(before, anyhot.astype(BF16)) + base_ref[...]
    rank = jnp.zeros((tm, LANES), F32)
    for k in range(TOP_K):
        rank = jnp.where(lane == k, jnp.sum(onehots[k] * prior, axis=-1, keepdims=True), rank)
    rank_ref[...] = rank.astype(jnp.int32)
    base_ref[...] += jnp.sum(anyhot, axis=0, keepdims=True)
    cnt_ref[...] = base_ref[...]


def _route_rank(idx_pad):
    T = idx_pad.shape[0]
    tm = min(ROUTE_ROWS, T)
    return pl.pallas_call(
        _route_rank_kernel,
        grid=(T // tm,),
        in_specs=[pl.BlockSpec((tm, LANES), lambda i: (i, 0))],
        out_specs=[pl.BlockSpec((tm, LANES), lambda i: (i, 0)), pl.BlockSpec((1, LANES), lambda i: (0, 0))],
        out_shape=[jax.ShapeDtypeStruct((T, LANES), jnp.int32), jax.ShapeDtypeStruct((1, LANES), F32)],
        scratch_shapes=[pltpu.VMEM((1, LANES), F32)],
        compiler_params=_cparams("arbitrary"),
        name="route_rank",
    )(idx_pad)


def _routing(idx_pad):
    T = idx_pad.shape[0]
    K = TOP_K
    N = T * K
    E = N_EXPERTS
    rows = MOE_ROWS
    rank_pad, counts_pad = _route_rank(idx_pad)
    counts = counts_pad[0, :E].astype(jnp.int32)
    starts = jnp.cumsum(counts) - counts
    padded = (counts + rows - 1) // rows * rows
    pad_ends = jnp.cumsum(padded)
    pad_starts = pad_ends - padded
    top_idx = idx_pad[:, :K]
    slots = jnp.take(pad_starts, top_idx) + rank_pad[:, :K]
    order = jnp.argsort(top_idx.reshape(-1), stable=True).astype(jnp.int32)
    P = N + E * rows
    nblk = P // rows
    blk_start = jnp.arange(nblk, dtype=jnp.int32) * rows
    block_e = jnp.minimum(jnp.sum((pad_ends[None, :] <= blk_start[:, None]).astype(jnp.int32), axis=1), E - 1)
    e_slot = jnp.repeat(block_e, rows)
    r_slot = jnp.arange(P, dtype=jnp.int32) - jnp.take(pad_starts, e_slot)
    src = jnp.clip(jnp.take(starts, e_slot) + r_slot, 0, N - 1)
    buf_tok = jnp.where(r_slot < jnp.take(counts, e_slot), jnp.take(order, src) // K, 0)
    n_valid = (pad_ends[-1] // rows).astype(jnp.int32).reshape(1)
    return slots, buf_tok, block_e, n_valid


def _layer(h2d, p2d, B, S, norm_mix_w, w_in, ml_igate_b, ml_fgate_b, ml_norm_w, sw_q_norm_w, sw_k_norm_w, sw_sinks,
           w_branch_a, w_branch_b, w_out, norm_ffn_w, w_router, b_router, w_expert_up, b_expert_up,
           w_expert_down, b_expert_down, norm_ple_w, w_ple_gate, w_ple_proj):
    T, D = h2d.shape
    H = ML_HEADS
    qk_w, v_w = H * ML_DQK, H * ML_DV
    sq_w, skv_w = SW_HEADS * SW_HD, SW_KV_HEADS * SW_HD
    g0 = 2 * qk_w + 2 * v_w
    s0 = g0 + 2 * H
    w_sq = w_in[:, s0:s0 + sq_w]
    w_skv = w_in[:, s0 + sq_w:s0 + sq_w + 2 * skv_w]
    w_gab = w_in[:, s0 + sq_w + 2 * skv_w:]
    w_main = jnp.concatenate([w_in[:, :g0], w_sq, w_gab, w_skv], axis=1).astype(BF16)
    w_gate = jnp.pad(w_in[:, g0:s0], ((0, 0), (0, LANES - 2 * H))).astype(BF16)
    sq_off = g0
    ga_off = sq_off + sq_w
    gb_off = ga_off + D
    sk_off = gb_off + D
    sv_off = sk_off + skv_w

    proj, gates = _in_proj(h2d, norm_mix_w.reshape(1, D), w_main, w_gate)

    gate_b = jnp.concatenate([ml_igate_b, ml_fgate_b]).astype(F32)
    bias_row = jnp.pad(gate_b, (0, LANES - 2 * H)).reshape(1, LANES)
    bias_col = gate_b.reshape(2 * H, 1)
    gates_t = gates[:, :2 * H].T
    h_ml = _mlstm(proj, gates, gates_t, bias_row, bias_col, ml_norm_w.reshape(1, v_w), B, S)

    h_sw = _swa(proj, sw_sinks.astype(F32), sw_q_norm_w.reshape(1, SW_HD), sw_k_norm_w.reshape(1, SW_HD),
                B, S, sq_off, sk_off, sv_off)

    mixed = _merge(h_ml, h_sw, w_branch_a.astype(BF16), w_branch_b.astype(BF16), proj, ga_off, gb_off)

    wr = jnp.pad(w_router, ((0, 0), (0, LANES - N_EXPERTS)))
    br = jnp.pad(b_router, (0, LANES - N_EXPERTS), constant_values=NEG_BIG).reshape(1, LANES)
    h1, xn, idx_pad, gate_pad = _out_proj(mixed, h2d, w_out.astype(BF16), norm_ffn_w.reshape(1, D), wr, br)

    slots, buf_tok, block_e, n_valid = _routing(idx_pad)
    E = N_EXPERTS
    ys = _experts(xn, buf_tok, block_e, n_valid,
                  w_expert_up.astype(BF16), b_expert_up.reshape(E, 1, -1),
                  w_expert_down.astype(BF16), b_expert_down.reshape(E, 1, -1))

    return _combine(ys, slots, h1, gate_pad, p2d, norm_ple_w.reshape(1, D),
                    w_ple_gate.astype(BF16), w_ple_proj.astype(BF16))


def kernel(x, p, norm_mix_w, w_in, ml_igate_b, ml_fgate_b, ml_norm_w, sw_q_norm_w, sw_k_norm_w, sw_sinks,
           w_branch_a, w_branch_b, w_out, norm_ffn_w, w_router, b_router, w_expert_up, b_expert_up,
           w_expert_down, b_expert_down, norm_ple_w, w_ple_gate, w_ple_proj):
    B, S, D = x.shape
    depth = p.shape[0]
    h = x.reshape(B * S, D)
    for i in range(depth):
        h = _layer(h, p[i].reshape(B * S, -1), B, S, norm_mix_w[i], w_in[i], ml_igate_b[i], ml_fgate_b[i],
                   ml_norm_w[i], sw_q_norm_w[i], sw_k_norm_w[i], sw_sinks[i], w_branch_a[i], w_branch_b[i],
                   w_out[i], norm_ffn_w[i], w_router[i], b_router[i], w_expert_up[i], b_expert_up[i],
                   w_expert_down[i], b_expert_down[i], norm_ple_w[i], w_ple_gate[i], w_ple_proj[i])
    return h.reshape(B, S, D)
```

```python
import functools

import jax
import jax.numpy as jnp
from jax import lax
from jax.experimental import pallas as pl
from jax.experimental.pallas import tpu as pltpu

F32 = jnp.float32
BF16 = jnp.bfloat16

EPS = 1e-6
ML_HEADS = 8
ML_DQK = 128
ML_DV = 256
GATE_CAP = 15.0
SW_HEADS = 32
SW_KV_HEADS = 4
SW_GROUP = SW_HEADS // SW_KV_HEADS
SW_HD = 64
WINDOW = 128
N_EXPERTS = 32
TOP_K = 4
SWIGLU_LIMIT = 7.0
SWIGLU_ALPHA = 1.702

LANES = 128
VMEM_LIMIT = 56 * 1024 * 1024

ML_CHUNK = 256
MOE_ROWS = 512
MOE_HALVES = 2
MOE_FF_CHUNK = 512
MOE_OUT_CHUNK = 1024
ROUTE_ROWS = 512
NEG_BIG = -1e30
LOG2_E = 1.4426950408889634


def _cparams(*sem):
    return pltpu.CompilerParams(dimension_semantics=sem, vmem_limit_bytes=VMEM_LIMIT)


def _dot(a, b):
    return jnp.dot(a, b, preferred_element_type=F32)


def _dot_nt(a, b):
    return lax.dot_general(a, b, (((1,), (1,)), ((), ())), preferred_element_type=F32)


def _dot_tn(a, b):
    return lax.dot_general(a, b, (((0,), (0,)), ((), ())), preferred_element_type=F32)


def _dot_f32(a, b):
    return jnp.dot(a, b, preferred_element_type=F32, precision=lax.Precision.HIGHEST)


def _rms(x, w):
    return x * lax.rsqrt(jnp.mean(x * x, axis=-1, keepdims=True) + EPS) * w


def _sigmoid(x):
    return 1.0 / (1.0 + jnp.exp(-x))


def _in_proj_kernel(x_ref, nw_ref, w_ref, wg_ref, o_ref, g_ref, xn_ref):
    @pl.when(pl.program_id(1) == 0)
    def _():
        xn = _rms(x_ref[...], nw_ref[...]).astype(BF16)
        xn_ref[...] = xn
        g_ref[...] = _dot(xn, wg_ref[...])

    o_ref[...] = _dot(xn_ref[...], w_ref[...]).astype(o_ref.dtype)


def _in_proj(x2d, norm_w, w_main, w_gate):
    T, D = x2d.shape
    N = w_main.shape[1]
    tm = min(1024, T)
    tn = 1280
    return pl.pallas_call(
        _in_proj_kernel,
        grid=(T // tm, N // tn),
        in_specs=[pl.BlockSpec((tm, D), lambda i, j: (i, 0)),
                  pl.BlockSpec((1, D), lambda i, j: (0, 0)),
                  pl.BlockSpec((D, tn), lambda i, j: (0, j)),
                  pl.BlockSpec((D, LANES), lambda i, j: (0, 0))],
        out_specs=[pl.BlockSpec((tm, tn), lambda i, j: (i, j)),
                   pl.BlockSpec((tm, LANES), lambda i, j: (i, 0))],
        out_shape=[jax.ShapeDtypeStruct((T, N), BF16), jax.ShapeDtypeStruct((T, LANES), F32)],
        scratch_shapes=[pltpu.VMEM((tm, D), BF16)],
        compiler_params=_cparams("arbitrary", "arbitrary"),
        name="in_proj",
    )(x2d, norm_w, w_main, w_gate)


def _log_sigmoid(z):
    return jnp.minimum(z, 0.0) - jnp.log1p(jnp.exp(-jnp.abs(z)))


def _soft_cap(z):
    return GATE_CAP * jnp.tanh(z / GATE_CAP)


def _mlstm_kernel(q_ref, k_ref, v_ref, o_ref, g_ref, gt_ref, brow_ref, bcol_ref, nw_ref,
                  out_ref, c_ref, n_ref, m_ref):
    L = q_ref.shape[0]
    H = ML_HEADS

    @pl.when(pl.program_id(1) == 0)
    def _():
        c_ref[...] = jnp.zeros_like(c_ref)
        n_ref[...] = jnp.zeros_like(n_ref)
        m_ref[...] = jnp.zeros_like(m_ref)

    g = g_ref[...] + brow_ref[...]
    gt = gt_ref[...] + bcol_ref[...]
    li_col = _soft_cap(g)
    lf_col = _log_sigmoid(_soft_cap(g))
    li_row = _soft_cap(gt[0:H, :])
    lf_row = _log_sigmoid(_soft_cap(gt[H:2 * H, :]))

    row = lax.broadcasted_iota(jnp.int32, (L, L), 0)
    col = lax.broadcasted_iota(jnp.int32, (L, L), 1)
    causal = col <= row
    tri = causal.astype(F32)
    tri_t = (row <= col).astype(F32)
    b_col = _dot_f32(tri, lf_col)
    b_row = _dot_f32(lf_row, tri_t)
    a_col = li_col[:, 0:H] - b_col[:, H:2 * H]
    a_row = li_row - b_row

    scale = ML_DQK ** -0.5
    for h in range(H):
        qh = q_ref[:, h * ML_DQK:(h + 1) * ML_DQK]
        kh = k_ref[:, h * ML_DQK:(h + 1) * ML_DQK]
        vh = v_ref[:, h * ML_DV:(h + 1) * ML_DV]
        m_prev = m_ref[h:h + 1, 0:1]
        c_prev = c_ref[h]
        n_prev = n_ref[h:h + 1, :]

        amat = jnp.where(causal, a_row[h:h + 1, :], -jnp.inf)
        big_m = jnp.maximum(jnp.max(amat, axis=-1, keepdims=True), m_prev)
        dmat = jnp.exp(amat - big_m)
        inter = jnp.exp(m_prev - big_m)

        s = _dot_nt(qh, kh) * dmat
        qf = qh.astype(F32)
        num = _dot(s.astype(BF16), vh) + inter * _dot(qh, c_prev.astype(BF16))
        den = jnp.sum(s, axis=-1, keepdims=True) + inter * jnp.sum(qf * n_prev, axis=-1, keepdims=True)
        floor = jnp.exp(-(b_col[:, H + h:H + h + 1] + big_m))
        cell = (scale * num) / jnp.maximum(scale * jnp.abs(den), floor)

        w_head = nw_ref[:, h * ML_DV:(h + 1) * ML_DV]
        og = _sigmoid(o_ref[:, h * ML_DV:(h + 1) * ML_DV].astype(F32))
        out_ref[:, h * ML_DV:(h + 1) * ML_DV] = (_rms(cell, w_head) * og).astype(out_ref.dtype)

        m_last = big_m[L - 1:L, :]
        w_col = jnp.exp(a_col[:, h:h + 1] - m_last)
        decay = jnp.exp(m_prev - m_last)
        kw = kh.astype(F32) * w_col
        c_ref[h] = decay * c_prev + _dot_tn(kw.astype(BF16), vh)
        n_ref[h:h + 1, :] = decay * n_prev + jnp.sum(kw, axis=0, keepdims=True)
        m_new = b_col[L - 1:L, H + h:H + h + 1] + m_last
        m_ref[h:h + 1, :] = jnp.broadcast_to(m_new, (1, LANES))


def _mlstm(proj, gates, gates_t, bias_row, bias_col, norm_w, B, S):
    T = B * S
    L = min(ML_CHUNK, S)
    nc = S // L
    qk_w = ML_HEADS * ML_DQK
    v_w = ML_HEADS * ML_DV
    assert 2 * qk_w == v_w
    rows = lambda b, c: b * nc + c
    return pl.pallas_call(
        _mlstm_kernel,
        grid=(B, nc),
        in_specs=[pl.BlockSpec((L, qk_w), lambda b, c: (rows(b, c), 0)),
                  pl.BlockSpec((L, qk_w), lambda b, c: (rows(b, c), 1)),
                  pl.BlockSpec((L, v_w), lambda b, c: (rows(b, c), 1)),
                  pl.BlockSpec((L, v_w), lambda b, c: (rows(b, c), 2)),
                  pl.BlockSpec((L, LANES), lambda b, c: (rows(b, c), 0)),
                  pl.BlockSpec((2 * ML_HEADS, L), lambda b, c: (0, rows(b, c))),
                  pl.BlockSpec((1, LANES), lambda b, c: (0, 0)),
                  pl.BlockSpec((2 * ML_HEADS, 1), lambda b, c: (0, 0)),
                  pl.BlockSpec((1, v_w), lambda b, c: (0, 0))],
        out_specs=pl.BlockSpec((L, v_w), lambda b, c: (rows(b, c), 0)),
        out_shape=jax.ShapeDtypeStruct((T, v_w), BF16),
        scratch_shapes=[pltpu.VMEM((ML_HEADS, ML_DQK, ML_DV), F32),
                        pltpu.VMEM((ML_HEADS, ML_DQK), F32),
                        pltpu.VMEM((ML_HEADS, LANES), F32)],
        compiler_params=_cparams("arbitrary", "arbitrary"),
        name="mlstm",
    )(proj, proj, proj, proj, gates, gates_t, bias_row, bias_col, norm_w)


def _swa_kernel(sink_ref, q_ref, kp_ref, kc_ref, vp_ref, vc_ref, qw_ref, kw_ref, bias_ref, out_ref):
    Q = q_ref.shape[0]
    HD = SW_HD
    assert LANES == 2 * HD

    gr = lax.broadcasted_iota(jnp.int32, (LANES, LANES), 0) // HD
    gc = lax.broadcasted_iota(jnp.int32, (LANES, LANES), 1) // HD
    gmat = (gr == gc).astype(BF16)

    def group_rms_chunks(x, gain):
        rows, width = x.shape
        nchunk = width // LANES
        x2 = x * x
        p0 = x2.astype(BF16)
        r1 = x2 - p0.astype(F32)
        p1 = r1.astype(BF16)
        p2 = (r1 - p1.astype(F32)).astype(BF16)
        pieces = [p[:, c * LANES:(c + 1) * LANES] for p in (p0, p1, p2) for c in range(nchunk)]
        sums = _dot(jnp.concatenate(pieces, axis=0), gmat)
        out = []
        for c in range(nchunk):
            ss = (sums[c * rows:(c + 1) * rows] + sums[(nchunk + c) * rows:(nchunk + c + 1) * rows]
                  + sums[(2 * nchunk + c) * rows:(2 * nchunk + c + 1) * rows])
            sl = slice(c * LANES, (c + 1) * LANES)
            out.append(x[:, sl] * lax.rsqrt(ss * (1.0 / HD) + EPS) * gain[:, sl])
        return out

    qn = [c.astype(BF16) for c in group_rms_chunks(q_ref[...].astype(F32), qw_ref[...] * (HD ** -0.5 * LOG2_E))]
    kn = group_rms_chunks(jnp.concatenate([kp_ref[...], kc_ref[...]], axis=0).astype(F32), kw_ref[...])
    vf = jnp.concatenate([vp_ref[...], vc_ref[...]], axis=0).astype(F32)
    lo_half = lax.broadcasted_iota(jnp.int32, (2 * Q, LANES), 1) < HD

    for g in range(SW_KV_HEADS):
        kc = kn[g // 2]
        vc = vf[:, (g // 2) * LANES:(g // 2 + 1) * LANES]
        kr = pltpu.roll(kc, HD, axis=1)
        vr = pltpu.roll(vc, HD, axis=1)
        k_lo, k_hi, v_lo, v_hi = (kc, kr, vc, vr) if g % 2 == 0 else (kr, kc, vr, vc)
        ka = jnp.where(lo_half, k_lo, 0.0).astype(BF16)
        kb = jnp.where(lo_half, 0.0, k_hi).astype(BF16)
        vv = jnp.concatenate([jnp.where(lo_half, v_lo, 0.0), jnp.where(lo_half, 0.0, v_hi)], axis=0).astype(BF16)
        nch = SW_GROUP // 2
        lhs = jnp.concatenate(qn[g * nch:(g + 1) * nch], axis=0)
        scores = (_dot_nt(lhs, ka), _dot_nt(lhs, kb))
        probs = ([], [])
        for m in range(nch):
            for par in range(2):
                h = g * SW_GROUP + 2 * m + par
                sink = sink_ref[h] * LOG2_E
                sc = scores[par][m * Q:(m + 1) * Q, :] + bias_ref[h]
                mx = jnp.maximum(jnp.max(sc, axis=-1, keepdims=True), sink)
                e = jnp.exp2(sc - mx)
                denom = jnp.sum(e, axis=-1, keepdims=True) + jnp.exp2(sink - mx)
                probs[par].append((e / denom).astype(BF16))
        p_all = jnp.concatenate([jnp.concatenate(probs[0], axis=0), jnp.concatenate(probs[1], axis=0)], axis=-1)
        o_all = _dot(p_all, vv)
        for m in range(nch):
            c0 = (g * nch + m) * LANES
            out_ref[:, c0:c0 + LANES] = o_all[m * Q:(m + 1) * Q, :].astype(out_ref.dtype)


def _swa(proj, sinks, q_norm_w, k_norm_w, B, S, q_off, k_off, v_off):
    T = B * S
    Q = WINDOW
    nb = S // Q
    q_w = SW_HEADS * SW_HD
    kv_w = SW_KV_HEADS * SW_HD
    cur = lambda b, n: b * nb + n
    prev = lambda b, n: b * nb + jnp.maximum(n - 1, 0)
    grid_spec = pltpu.PrefetchScalarGridSpec(
        num_scalar_prefetch=1,
        grid=(B, nb),
        in_specs=[pl.BlockSpec((Q, q_w), lambda b, n, s: (cur(b, n), q_off // q_w)),
                  pl.BlockSpec((Q, kv_w), lambda b, n, s: (prev(b, n), k_off // kv_w)),
                  pl.BlockSpec((Q, kv_w), lambda b, n, s: (cur(b, n), k_off // kv_w)),
                  pl.BlockSpec((Q, kv_w), lambda b, n, s: (prev(b, n), v_off // kv_w)),
                  pl.BlockSpec((Q, kv_w), lambda b, n, s: (cur(b, n), v_off // kv_w)),
                  pl.BlockSpec((1, q_w), lambda b, n, s: (0, 0)),
                  pl.BlockSpec((1, kv_w), lambda b, n, s: (0, 0)),
                  pl.BlockSpec((None, SW_HEADS, Q, 2 * Q), lambda b, n, s: (jnp.minimum(n, 1), 0, 0, 0))],
        out_specs=pl.BlockSpec((Q, q_w), lambda b, n, s: (cur(b, n), 0)),
    )
    q_gain = jnp.tile(q_norm_w.reshape(1, SW_HD), (1, SW_HEADS))
    k_gain = jnp.tile(k_norm_w.reshape(1, SW_HD), (1, SW_KV_HEADS))
    qi = jnp.arange(Q, dtype=jnp.int32)[:, None]
    kj = jnp.arange(2 * Q, dtype=jnp.int32)[None, :]
    dist = qi + Q - kj
    in_window = (dist >= 0) & (dist < WINDOW)
    slopes = 2.0 ** (-8.0 * jnp.arange(1, SW_HEADS + 1, dtype=F32) / SW_HEADS)
    alibi = -(slopes * LOG2_E)[:, None, None] * dist.astype(F32)[None]
    bias = jnp.stack([jnp.where(in_window & (kj >= Q), alibi, -jnp.inf), jnp.where(in_window, alibi, -jnp.inf)])
    return pl.pallas_call(
        _swa_kernel,
        grid_spec=grid_spec,
        out_shape=jax.ShapeDtypeStruct((T, q_w), BF16),
        compiler_params=_cparams("arbitrary", "arbitrary"),
        name="swa",
    )(sinks, proj, proj, proj, proj, proj, q_gain, k_gain, bias)


def _merge_kernel(a_ref, b_ref, wa_ref, wb_ref, ga_ref, gb_ref, o_ref):
    ya = _dot(a_ref[...], wa_ref[...])
    yb = _dot(b_ref[...], wb_ref[...])
    mixed = _sigmoid(ga_ref[...].astype(F32)) * ya + _sigmoid(gb_ref[...].astype(F32)) * yb
    o_ref[...] = mixed.astype(o_ref.dtype)


def _merge(ha, hb, wa, wb, proj, ga_off, gb_off):
    T, D = ha.shape
    N = wa.shape[1]
    tm = min(1024, T)
    tn = 1024
    return pl.pallas_call(
        _merge_kernel,
        grid=(T // tm, N // tn),
        in_specs=[pl.BlockSpec((tm, D), lambda i, j: (i, 0)),
                  pl.BlockSpec((tm, D), lambda i, j: (i, 0)),
                  pl.BlockSpec((D, tn), lambda i, j: (0, j)),
                  pl.BlockSpec((D, tn), lambda i, j: (0, j)),
                  pl.BlockSpec((tm, tn), lambda i, j: (i, ga_off // tn + j)),
                  pl.BlockSpec((tm, tn), lambda i, j: (i, gb_off // tn + j))],
        out_specs=pl.BlockSpec((tm, tn), lambda i, j: (i, j)),
        out_shape=jax.ShapeDtypeStruct((T, N), BF16),
        compiler_params=_cparams("arbitrary", "arbitrary"),
        name="merge",
    )(ha, hb, wa, wb, proj, proj)


def _out_proj_kernel(mix_ref, x_ref, wo_ref, nw_ref, wrh_ref, wrl_ref, br_ref, h_ref, xn_ref, idx_ref, gate_ref):
    h = x_ref[...] + _dot(mix_ref[...], wo_ref[...])
    h_ref[...] = h
    xn = _rms(h, nw_ref[...])
    xn_ref[...] = xn
    xn_hi = xn.astype(BF16)
    xn_lo = (xn - xn_hi.astype(F32)).astype(BF16)
    logits = (_dot(xn_hi, wrh_ref[...]) + (_dot(xn_hi, wrl_ref[...]) + _dot(xn_lo, wrh_ref[...]))
              + br_ref[...])
    tm = logits.shape[0]
    lane_i = lax.broadcasted_iota(jnp.int32, (tm, LANES), 1)
    lane = lane_i.astype(F32)
    vals, idxs = [], []
    for _ in range(TOP_K):
        mx = jnp.max(logits, axis=-1, keepdims=True)
        ix = jnp.min(jnp.where(logits == mx, lane, float(LANES)), axis=-1, keepdims=True)
        vals.append(mx)
        idxs.append(ix)
        logits = jnp.where(lane == ix, -jnp.inf, logits)
    es = [jnp.exp(v - vals[0]) for v in vals]
    tot = es[0]
    for e in es[1:]:
        tot = tot + e
    idx_out = jnp.zeros((tm, LANES), F32)
    gate_out = jnp.zeros((tm, LANES), F32)
    for k in range(TOP_K):
        idx_out = jnp.where(lane_i == k, idxs[k], idx_out)
        gate_out = jnp.where(lane_i == k, es[k] / tot, gate_out)
    idx_ref[...] = idx_out.astype(jnp.int32)
    gate_ref[...] = gate_out


def _out_proj(mixed, x2d, w_out, norm_w, w_router, b_router):
    T, D = x2d.shape
    tm = min(512, T)
    row = lambda i: (i, 0)
    const = lambda i: (0, 0)
    wr_hi = w_router.astype(BF16)
    wr_lo = (w_router - wr_hi.astype(F32)).astype(BF16)
    return pl.pallas_call(
        _out_proj_kernel,
        grid=(T // tm,),
        in_specs=[pl.BlockSpec((tm, D), row), pl.BlockSpec((tm, D), row),
                  pl.BlockSpec((D, D), const), pl.BlockSpec((1, D), const),
                  pl.BlockSpec((D, LANES), const), pl.BlockSpec((D, LANES), const),
                  pl.BlockSpec((1, LANES), const)],
        out_specs=[pl.BlockSpec((tm, D), row), pl.BlockSpec((tm, D), row),
                   pl.BlockSpec((tm, LANES), row), pl.BlockSpec((tm, LANES), row)],
        out_shape=[jax.ShapeDtypeStruct((T, D), F32), jax.ShapeDtypeStruct((T, D), F32),
                   jax.ShapeDtypeStruct((T, LANES), jnp.int32), jax.ShapeDtypeStruct((T, LANES), F32)],
        compiler_params=_cparams("arbitrary"),
        name="out_proj",
    )(mixed, x2d, w_out, norm_w, wr_hi, wr_lo, b_router)


def _gather_copy(src_hbm, dst_buf, sem, src_row, dst_row):
    return pltpu.make_async_copy(src_hbm.at[pl.ds(src_row, 1), :], dst_buf.at[pl.ds(dst_row, 1), :], sem)


def _expert_kernel(be_ref, nh_ref, nv_ref, tok_ref, tokn_ref, x_hbm, wg_ref, wl_ref, bg_ref, bl_ref, wd_ref, bd_ref,
                   o_ref, xbuf, xb, act, sem):
    i = pl.program_id(0)
    c = pl.program_id(1)
    n_valid = nv_ref[0]
    halves = nh_ref[i]
    rows = xb.shape[0]
    half = rows // MOE_HALVES
    nu = act.shape[0]
    nd = xb.shape[1] // o_ref.shape[1]
    fc = act.shape[2]

    @pl.when((c == 0) & (i == 0))
    def _():
        def body(r, carry):
            _gather_copy(x_hbm, xbuf, sem, tok_ref[0, 0, r], r).start()
            return carry
        lax.fori_loop(0, rows, body, 0, unroll=8)

    @pl.when((c == 0) & (i <= n_valid))
    def _():
        pltpu.make_async_copy(x_hbm.at[pl.ds(0, rows), :], xbuf, sem).wait()

    @pl.when((c == 0) & (i < n_valid))
    def _():
        xb[...] = xbuf[...].astype(BF16)

    def request_next_rows(first, count):
        dst = xbuf.at[pl.ds(pl.multiple_of(first, 8), count), :]
        for r in range(count):
            _gather_copy(x_hbm, dst, sem, tokn_ref[0, 0, first + r], r).start()

    per_up = rows // nu
    assert nu * per_up == rows and per_up % 8 == 0

    def up(hf):
        rs = slice(hf * half, (hf + 1) * half)
        x = xb[rs, :]
        h_glu = jnp.minimum(_dot(x, wg_ref[...]) + bg_ref[...], SWIGLU_LIMIT)
        h_lin = jnp.clip(_dot(x, wl_ref[...]) + bl_ref[...], -SWIGLU_LIMIT, SWIGLU_LIMIT)
        act[c, rs, :] = (h_glu * _sigmoid(SWIGLU_ALPHA * h_glu) * (h_lin + 1.0)).astype(BF16)

    def down(hf):
        rs = slice(hf * half, (hf + 1) * half)
        y = _dot(act[0, rs, :], wd_ref[0:fc, :])
        for u in range(1, nu):
            y = y + _dot(act[u, rs, :], wd_ref[u * fc:(u + 1) * fc, :])
        o_ref[rs, :] = y + bd_ref[...]

    @pl.when((halves > 0) & (c < nu))
    def _():
        request_next_rows(c * per_up, per_up)
        up(0)

    @pl.when((halves > 0) & (c >= nu))
    def _():
        down(0)

    for hf in range(1, MOE_HALVES):
        @pl.when((halves > hf) & (c < nu))
        def _():
            up(hf)

        @pl.when((halves > hf) & (c >= nu))
        def _():
            down(hf)

        @pl.when((halves <= hf) & (c >= nu))
        def _():
            rs = slice(hf * half, (hf + 1) * half)
            o_ref[rs, :] = jnp.zeros((half, o_ref.shape[1]), o_ref.dtype)

    @pl.when((halves == 0) & (c >= nu))
    def _():
        o_ref[0:half, :] = jnp.zeros((half, o_ref.shape[1]), o_ref.dtype)


def _experts(xp, buf_tok, block_e, block_halves, n_valid, w_up, b_up, w_down, b_down):
    E, D, F2 = w_up.shape
    F = F2 // 2
    rows = MOE_ROWS * MOE_HALVES
    fc = MOE_FF_CHUNK
    dn = MOE_OUT_CHUNK
    nu = F // fc
    nd = D // dn
    nblk = buf_tok.shape[0] // rows
    tok3 = buf_tok.reshape(nblk, 1, rows)
    cu = lambda i, c, nv: jnp.where(i < nv[0], jnp.minimum(c, nu - 1), nu - 1)
    cd = lambda i, c, nv: jnp.where(i < nv[0], jnp.maximum(c - nu, 0), nd - 1)
    co = lambda c: jnp.maximum(c - nu, 0)
    grid_spec = pltpu.PrefetchScalarGridSpec(
        num_scalar_prefetch=3,
        grid=(nblk, nu + nd),
        in_specs=[
            pl.BlockSpec((1, 1, rows), lambda i, c, be, nh, nv: (i, 0, 0), memory_space=pltpu.SMEM),
            pl.BlockSpec((1, 1, rows), lambda i, c, be, nh, nv: (jnp.minimum(i + 1, nblk - 1), 0, 0),
                         memory_space=pltpu.SMEM),
            pl.BlockSpec(memory_space=pl.ANY),
            pl.BlockSpec((None, D, fc), lambda i, c, be, nh, nv: (be[i], 0, cu(i, c, nv))),
            pl.BlockSpec((None, D, fc), lambda i, c, be, nh, nv: (be[i], 0, nu + cu(i, c, nv))),
            pl.BlockSpec((None, 1, fc), lambda i, c, be, nh, nv: (be[i], 0, cu(i, c, nv))),
            pl.BlockSpec((None, 1, fc), lambda i, c, be, nh, nv: (be[i], 0, nu + cu(i, c, nv))),
            pl.BlockSpec((None, F, dn), lambda i, c, be, nh, nv: (be[i], 0, cd(i, c, nv))),
            pl.BlockSpec((None, 1, dn), lambda i, c, be, nh, nv: (be[i], 0, cd(i, c, nv))),
        ],
        out_specs=pl.BlockSpec((rows, dn), lambda i, c, be, nh, nv: (i, co(c))),
        scratch_shapes=[pltpu.VMEM((rows, D), F32), pltpu.VMEM((rows, D), BF16),
                        pltpu.VMEM((nu, rows, fc), BF16), pltpu.SemaphoreType.DMA],
    )
    return pl.pallas_call(
        _expert_kernel,
        grid_spec=grid_spec,
        out_shape=jax.ShapeDtypeStruct((nblk * rows, D), F32),
        compiler_params=_cparams("arbitrary", "arbitrary"),
        name="experts",
    )(block_e, block_halves, n_valid, tok3, tok3, xp, w_up, w_up, b_up, b_up, w_down, b_down)


def _combine_kernel(slot_ref, slotn_ref, y_hbm, h_ref, gate_ref, p_ref, nw_ref, wg_ref, wp_ref, o_ref, ybuf, sem):
    i = pl.program_id(0)
    nsteps = pl.num_programs(0)
    tm = h_ref.shape[0]
    n = TOP_K * tm
    s = i % 2

    def wait_rows(b):
        pltpu.make_async_copy(y_hbm.at[pl.ds(0, n), :], ybuf.at[b], sem.at[b]).wait()

    @pl.when(i == 0)
    def _():
        def body(r, carry):
            _gather_copy(y_hbm, ybuf.at[0], sem.at[0], slot_ref[0, 0, r], r).start()
            return carry
        lax.fori_loop(0, n, body, 0, unroll=8)

    wait_rows(s)
    nxt = ybuf.at[1 - s]
    for r in range(n):
        _gather_copy(y_hbm, nxt, sem.at[1 - s], slotn_ref[0, 0, r], r).start()

    h = h_ref[...]
    gates = gate_ref[...]
    for k in range(TOP_K):
        h = h + gates[:, k:k + 1] * ybuf[s, k * tm:(k + 1) * tm, :]
    gate = _sigmoid(_dot(_rms(h, nw_ref[...]).astype(BF16), wg_ref[...]))
    o_ref[...] = h + gate * _dot(p_ref[...].astype(BF16), wp_ref[...])

    @pl.when(i == nsteps - 1)
    def _():
        wait_rows(1 - s)


def _combine(ys, slots, h1, gates, p2d, norm_w, w_gate, w_proj):
    T, D = h1.shape
    PD = p2d.shape[1]
    tm = min(256, T)
    nsteps = T // tm
    slot3 = slots.reshape(nsteps, tm, TOP_K).transpose(0, 2, 1).reshape(nsteps, 1, TOP_K * tm)
    row = lambda i: (i, 0)
    const = lambda i: (0, 0)
    return pl.pallas_call(
        _combine_kernel,
        grid=(nsteps,),
        in_specs=[pl.BlockSpec((1, 1, TOP_K * tm), lambda i: (i, 0, 0), memory_space=pltpu.SMEM),
                  pl.BlockSpec((1, 1, TOP_K * tm), lambda i: (jnp.minimum(i + 1, nsteps - 1), 0, 0),
                               memory_space=pltpu.SMEM),
                  pl.BlockSpec(memory_space=pl.ANY),
                  pl.BlockSpec((tm, D), row), pl.BlockSpec((tm, LANES), row), pl.BlockSpec((tm, PD), row),
                  pl.BlockSpec((1, D), const), pl.BlockSpec((D, D), const), pl.BlockSpec((PD, D), const)],
        out_specs=pl.BlockSpec((tm, D), row),
        out_shape=jax.ShapeDtypeStruct((T, D), F32),
        scratch_shapes=[pltpu.VMEM((2, TOP_K * tm, D), F32), pltpu.SemaphoreType.DMA((2,))],
        compiler_params=_cparams("arbitrary"),
        name="combine",
    )(slot3, slot3, ys, h1, gates, p2d, norm_w, w_gate, w_proj)


def _route_rank_kernel(idx_ref, rank_ref, cnt_ref, base_ref):
    @pl.when(pl.program_id(0) == 0)
    def _():
        base_ref[...] = jnp.zeros_like(base_ref)

    tm = idx_ref.shape[0]
    lane = lax.broadcasted_iota(jnp.int32, (tm, LANES), 1)
    idx = idx_ref[...]
    onehots = [(lane == idx[:, k:k + 1]).astype(F32) for k in range(TOP_K)]
    anyhot = onehots[0]
    for oh in onehots[1:]:
        anyhot = anyhot + oh
    r = lax.broadcasted_iota(jnp.int32, (tm, tm), 0)
    c = lax.broadcasted_iota(jnp.int32, (tm, tm), 1)
    before = (c < r).astype(BF16)
    prior = _dot(before, anyhot.astype(BF16)) + base_ref[...]
    rank = jnp.zeros((tm, LANES), F32)
    for k in range(TOP_K):
        rank = jnp.where(lane == k, jnp.sum(onehots[k] * prior, axis=-1, keepdims=True), rank)
    rank_ref[...] = rank.astype(jnp.int32)
    base_ref[...] += jnp.sum(anyhot, axis=0, keepdims=True)
    cnt_ref[...] = base_ref[...]


def _route_rank(idx_pad):
    T = idx_pad.shape[0]
    tm = min(ROUTE_ROWS, T)
    return pl.pallas_call(
        _route_rank_kernel,
        grid=(T // tm,),
        in_specs=[pl.BlockSpec((tm, LANES), lambda i: (i, 0))],
        out_specs=[pl.BlockSpec((tm, LANES), lambda i: (i, 0)), pl.BlockSpec((1, LANES), lambda i: (0, 0))],
        out_shape=[jax.ShapeDtypeStruct((T, LANES), jnp.int32), jax.ShapeDtypeStruct((1, LANES), F32)],
        scratch_shapes=[pltpu.VMEM((1, LANES), F32)],
        compiler_params=_cparams("arbitrary"),
        name="route_rank",
    )(idx_pad)


def _routing(idx_pad):
    T = idx_pad.shape[0]
    K = TOP_K
    N = T * K
    E = N_EXPERTS
    rows = MOE_ROWS * MOE_HALVES
    rank_pad, counts_pad = _route_rank(idx_pad)
    counts = counts_pad[0, :E].astype(jnp.int32)
    starts = jnp.cumsum(counts) - counts
    padded = (counts + rows - 1) // rows * rows
    pad_ends = jnp.cumsum(padded)
    pad_starts = pad_ends - padded
    top_idx = idx_pad[:, :K]
    slots = jnp.take(pad_starts, top_idx) + rank_pad[:, :K]
    order = jnp.argsort(top_idx.reshape(-1), stable=True).astype(jnp.int32)
    P = N + E * rows
    nblk = P // rows
    blk_start = jnp.arange(nblk, dtype=jnp.int32) * rows
    block_e = jnp.minimum(jnp.sum((pad_ends[None, :] <= blk_start[:, None]).astype(jnp.int32), axis=1), E - 1)
    e_slot = jnp.repeat(block_e, rows)
    r_slot = jnp.arange(P, dtype=jnp.int32) - jnp.take(pad_starts, e_slot)
    src = jnp.clip(jnp.take(starts, e_slot) + r_slot, 0, N - 1)
    buf_tok = jnp.where(r_slot < jnp.take(counts, e_slot), jnp.take(order, src) // K, 0)
    n_valid = (pad_ends[-1] // rows).astype(jnp.int32).reshape(1)
    remaining = jnp.take(counts, block_e) - (blk_start - jnp.take(pad_starts, block_e))
    block_halves = jnp.clip((remaining + MOE_ROWS - 1) // MOE_ROWS, 0, MOE_HALVES).astype(jnp.int32)
    return slots, buf_tok, block_e, block_halves, n_valid


def _layer(h2d, p2d, B, S, norm_mix_w, w_in, ml_igate_b, ml_fgate_b, ml_norm_w, sw_q_norm_w, sw_k_norm_w, sw_sinks,
           w_branch_a, w_branch_b, w_out, norm_ffn_w, w_router, b_router, w_expert_up, b_expert_up,
           w_expert_down, b_expert_down, norm_ple_w, w_ple_gate, w_ple_proj):
    T, D = h2d.shape
    H = ML_HEADS
    qk_w, v_w = H * ML_DQK, H * ML_DV
    sq_w, skv_w = SW_HEADS * SW_HD, SW_KV_HEADS * SW_HD
    g0 = 2 * qk_w + 2 * v_w
    s0 = g0 + 2 * H
    w_sq = w_in[:, s0:s0 + sq_w]
    w_skv = w_in[:, s0 + sq_w:s0 + sq_w + 2 * skv_w]
    w_gab = w_in[:, s0 + sq_w + 2 * skv_w:]
    w_main = jnp.concatenate([w_in[:, :g0], w_sq, w_gab, w_skv], axis=1).astype(BF16)
    w_gate = jnp.pad(w_in[:, g0:s0], ((0, 0), (0, LANES - 2 * H))).astype(BF16)
    sq_off = g0
    ga_off = sq_off + sq_w
    gb_off = ga_off + D
    sk_off = gb_off + D
    sv_off = sk_off + skv_w

    proj, gates = _in_proj(h2d, norm_mix_w.reshape(1, D), w_main, w_gate)

    gate_b = jnp.concatenate([ml_igate_b, ml_fgate_b]).astype(F32)
    bias_row = jnp.pad(gate_b, (0, LANES - 2 * H)).reshape(1, LANES)
    bias_col = gate_b.reshape(2 * H, 1)
    gates_t = gates[:, :2 * H].T
    h_ml = _mlstm(proj, gates, gates_t, bias_row, bias_col, ml_norm_w.reshape(1, v_w), B, S)

    h_sw = _swa(proj, sw_sinks.astype(F32), sw_q_norm_w.reshape(1, SW_HD), sw_k_norm_w.reshape(1, SW_HD),
                B, S, sq_off, sk_off, sv_off)

    mixed = _merge(h_ml, h_sw, w_branch_a.astype(BF16), w_branch_b.astype(BF16), proj, ga_off, gb_off)

    wr = jnp.pad(w_router, ((0, 0), (0, LANES - N_EXPERTS)))
    br = jnp.pad(b_router, (0, LANES - N_EXPERTS), constant_values=NEG_BIG).reshape(1, LANES)
    h1, xn, idx_pad, gate_pad = _out_proj(mixed, h2d, w_out.astype(BF16), norm_ffn_w.reshape(1, D), wr, br)

    slots, buf_tok, block_e, block_halves, n_valid = _routing(idx_pad)
    E = N_EXPERTS
    ys = _experts(xn, buf_tok, block_e, block_halves, n_valid,
                  w_expert_up.astype(BF16), b_expert_up.reshape(E, 1, -1),
                  w_expert_down.astype(BF16), b_expert_down.reshape(E, 1, -1))

    return _combine(ys, slots, h1, gate_pad, p2d, norm_ple_w.reshape(1, D),
                    w_ple_gate.astype(BF16), w_ple_proj.astype(BF16))


def kernel(x, p, norm_mix_w, w_in, ml_igate_b, ml_fgate_b, ml_norm_w, sw_q_norm_w, sw_k_norm_w, sw_sinks,
           w_branch_a, w_branch_b, w_out, norm_ffn_w, w_router, b_router, w_expert_up, b_expert_up,
           w_expert_down, b_expert_down, norm_ple_w, w_ple_gate, w_ple_proj):
    B, S, D = x.shape
    depth = p.shape[0]
    h = x.reshape(B * S, D)
    for i in range(depth):
        h = _layer(h, p[i].reshape(B * S, -1), B, S, norm_mix_w[i], w_in[i], ml_igate_b[i], ml_fgate_b[i],
                   ml_norm_w[i], sw_q_norm_w[i], sw_k_norm_w[i], sw_sinks[i], w_branch_a[i], w_branch_b[i],
                   w_out[i], norm_ffn_w[i], w_router[i], b_router[i], w_expert_up[i], b_expert_up[i],
                   w_expert_down[i], b_expert_down[i], norm_ple_w[i], w_ple_gate[i], w_ple_proj[i])
    return h.reshape(B, S, D)
```

```python
import functools

import jax
import jax.numpy as jnp
from jax import lax
from jax.experimental import pallas as pl
from jax.experimental.pallas import tpu as pltpu

F32 = jnp.float32
BF16 = jnp.bfloat16

EPS = 1e-6
ML_HEADS = 8
ML_DQK = 128
ML_DV = 256
GATE_CAP = 15.0
SW_HEADS = 32
SW_KV_HEADS = 4
SW_GROUP = SW_HEADS // SW_KV_HEADS
SW_HD = 64
WINDOW = 128
N_EXPERTS = 32
TOP_K = 4
SWIGLU_LIMIT = 7.0
SWIGLU_ALPHA = 1.702

LANES = 128
VMEM_LIMIT = 56 * 1024 * 1024

ML_CHUNK = 256
MOE_ROWS = 512
MOE_HALVES = 2
MOE_FF_CHUNK = 512
MOE_OUT_CHUNK = 1024
ROUTE_ROWS = 512
NEG_BIG = -1e30
LOG2_E = 1.4426950408889634


def _cparams(*sem):
    return pltpu.CompilerParams(dimension_semantics=sem, vmem_limit_bytes=VMEM_LIMIT)


def _dot(a, b):
    return jnp.dot(a, b, preferred_element_type=F32)


def _dot_nt(a, b):
    return lax.dot_general(a, b, (((1,), (1,)), ((), ())), preferred_element_type=F32)


def _dot_tn(a, b):
    return lax.dot_general(a, b, (((0,), (0,)), ((), ())), preferred_element_type=F32)


def _dot_f32(a, b):
    return jnp.dot(a, b, preferred_element_type=F32, precision=lax.Precision.HIGHEST)


def _rms(x, w):
    return x * lax.rsqrt(jnp.mean(x * x, axis=-1, keepdims=True) + EPS) * w


def _sigmoid(x):
    return 1.0 / (1.0 + jnp.exp(-x))


def _in_proj_kernel(x_ref, nw_ref, w_ref, wg_ref, o_ref, g_ref, xn_ref):
    @pl.when(pl.program_id(1) == 0)
    def _():
        xn = _rms(x_ref[...], nw_ref[...]).astype(BF16)
        xn_ref[...] = xn
        g_ref[...] = _dot(xn, wg_ref[...])

    o_ref[...] = _dot(xn_ref[...], w_ref[...]).astype(o_ref.dtype)


def _in_proj(x2d, norm_w, w_main, w_gate):
    T, D = x2d.shape
    N = w_main.shape[1]
    tm = min(1024, T)
    tn = 1280
    return pl.pallas_call(
        _in_proj_kernel,
        grid=(T // tm, N // tn),
        in_specs=[pl.BlockSpec((tm, D), lambda i, j: (i, 0)),
                  pl.BlockSpec((1, D), lambda i, j: (0, 0)),
                  pl.BlockSpec((D, tn), lambda i, j: (0, j)),
                  pl.BlockSpec((D, LANES), lambda i, j: (0, 0))],
        out_specs=[pl.BlockSpec((tm, tn), lambda i, j: (i, j)),
                   pl.BlockSpec((tm, LANES), lambda i, j: (i, 0))],
        out_shape=[jax.ShapeDtypeStruct((T, N), BF16), jax.ShapeDtypeStruct((T, LANES), F32)],
        scratch_shapes=[pltpu.VMEM((tm, D), BF16)],
        compiler_params=_cparams("arbitrary", "arbitrary"),
        name="in_proj",
    )(x2d, norm_w, w_main, w_gate)


def _log_sigmoid(z):
    return jnp.minimum(z, 0.0) - jnp.log1p(jnp.exp(-jnp.abs(z)))


def _soft_cap(z):
    return GATE_CAP * jnp.tanh(z / GATE_CAP)


def _mlstm_kernel(q_ref, k_ref, v_ref, o_ref, g_ref, gt_ref, brow_ref, bcol_ref, nw_ref, wsrc_ref,
                  out_ref, wdst_ref, c_ref, n_ref, m_ref):
    wdst_ref[...] = wsrc_ref[...].astype(wdst_ref.dtype)

    L = q_ref.shape[0]
    H = ML_HEADS

    @pl.when(pl.program_id(1) == 0)
    def _():
        c_ref[...] = jnp.zeros_like(c_ref)
        n_ref[...] = jnp.zeros_like(n_ref)
        m_ref[...] = jnp.zeros_like(m_ref)

    g = g_ref[...] + brow_ref[...]
    gt = gt_ref[...] + bcol_ref[...]
    li_col = _soft_cap(g)
    lf_col = _log_sigmoid(_soft_cap(g))
    li_row = _soft_cap(gt[0:H, :])
    lf_row = _log_sigmoid(_soft_cap(gt[H:2 * H, :]))

    row = lax.broadcasted_iota(jnp.int32, (L, L), 0)
    col = lax.broadcasted_iota(jnp.int32, (L, L), 1)
    causal = col <= row
    tri = causal.astype(F32)
    tri_t = (row <= col).astype(F32)
    b_col = _dot_f32(tri, lf_col)
    b_row = _dot_f32(lf_row, tri_t)
    a_col = li_col[:, 0:H] - b_col[:, H:2 * H]
    a_row = li_row - b_row

    scale = ML_DQK ** -0.5
    for h in range(H):
        qh = q_ref[:, h * ML_DQK:(h + 1) * ML_DQK]
        kh = k_ref[:, h * ML_DQK:(h + 1) * ML_DQK]
        vh = v_ref[:, h * ML_DV:(h + 1) * ML_DV]
        m_prev = m_ref[h:h + 1, 0:1]
        c_prev = c_ref[h]
        n_prev = n_ref[h:h + 1, :]

        amat = jnp.where(causal, a_row[h:h + 1, :], -jnp.inf)
        big_m = jnp.maximum(jnp.max(amat, axis=-1, keepdims=True), m_prev)
        dmat = jnp.exp(amat - big_m)
        inter = jnp.exp(m_prev - big_m)

        s = _dot_nt(qh, kh) * dmat
        qf = qh.astype(F32)
        num = _dot(s.astype(BF16), vh) + inter * _dot(qh, c_prev.astype(BF16))
        den = jnp.sum(s, axis=-1, keepdims=True) + inter * jnp.sum(qf * n_prev, axis=-1, keepdims=True)
        floor = jnp.exp(-(b_col[:, H + h:H + h + 1] + big_m))
        cell = (scale * num) / jnp.maximum(scale * jnp.abs(den), floor)

        w_head = nw_ref[:, h * ML_DV:(h + 1) * ML_DV]
        og = _sigmoid(o_ref[:, h * ML_DV:(h + 1) * ML_DV].astype(F32))
        out_ref[:, h * ML_DV:(h + 1) * ML_DV] = (_rms(cell, w_head) * og).astype(out_ref.dtype)

        m_last = big_m[L - 1:L, :]
        w_col = jnp.exp(a_col[:, h:h + 1] - m_last)
        decay = jnp.exp(m_prev - m_last)
        kw = kh.astype(F32) * w_col
        c_ref[h] = decay * c_prev + _dot_tn(kw.astype(BF16), vh)
        n_ref[h:h + 1, :] = decay * n_prev + jnp.sum(kw, axis=0, keepdims=True)
        m_new = b_col[L - 1:L, H + h:H + h + 1] + m_last
        m_ref[h:h + 1, :] = jnp.broadcast_to(m_new, (1, LANES))


def _mlstm(proj, gates, gates_t, bias_row, bias_col, norm_w, cast_src, B, S):
    T = B * S
    L = min(ML_CHUNK, S)
    nc = S // L
    cast_rows = cast_src.shape[0] // (B * nc)
    assert cast_rows * B * nc == cast_src.shape[0]
    cast_spec = pl.BlockSpec((cast_rows, cast_src.shape[1]), lambda b, c: (b * nc + c, 0))
    qk_w = ML_HEADS * ML_DQK
    v_w = ML_HEADS * ML_DV
    assert 2 * qk_w == v_w
    rows = lambda b, c: b * nc + c
    return pl.pallas_call(
        _mlstm_kernel,
        grid=(B, nc),
        in_specs=[pl.BlockSpec((L, qk_w), lambda b, c: (rows(b, c), 0)),
                  pl.BlockSpec((L, qk_w), lambda b, c: (rows(b, c), 1)),
                  pl.BlockSpec((L, v_w), lambda b, c: (rows(b, c), 1)),
                  pl.BlockSpec((L, v_w), lambda b, c: (rows(b, c), 2)),
                  pl.BlockSpec((L, LANES), lambda b, c: (rows(b, c), 0)),
                  pl.BlockSpec((2 * ML_HEADS, L), lambda b, c: (0, rows(b, c))),
                  pl.BlockSpec((1, LANES), lambda b, c: (0, 0)),
                  pl.BlockSpec((2 * ML_HEADS, 1), lambda b, c: (0, 0)),
                  pl.BlockSpec((1, v_w), lambda b, c: (0, 0)),
                  cast_spec],
        out_specs=[pl.BlockSpec((L, v_w), lambda b, c: (rows(b, c), 0)), cast_spec],
        out_shape=[jax.ShapeDtypeStruct((T, v_w), BF16), jax.ShapeDtypeStruct(cast_src.shape, BF16)],
        scratch_shapes=[pltpu.VMEM((ML_HEADS, ML_DQK, ML_DV), F32),
                        pltpu.VMEM((ML_HEADS, ML_DQK), F32),
                        pltpu.VMEM((ML_HEADS, LANES), F32)],
        compiler_params=_cparams("arbitrary", "arbitrary"),
        name="mlstm",
    )(proj, proj, proj, proj, gates, gates_t, bias_row, bias_col, norm_w, cast_src)


def _swa_kernel(sink_ref, q_ref, kp_ref, kc_ref, vp_ref, vc_ref, qw_ref, kw_ref, bias_ref, wsrc_ref,
                out_ref, wdst_ref):
    wdst_ref[...] = wsrc_ref[...].astype(wdst_ref.dtype)
    Q = q_ref.shape[0]
    HD = SW_HD
    assert LANES == 2 * HD

    gr = lax.broadcasted_iota(jnp.int32, (LANES, LANES), 0) // HD
    gc = lax.broadcasted_iota(jnp.int32, (LANES, LANES), 1) // HD
    gmat = (gr == gc).astype(BF16)

    def group_rms_chunks(x, gain):
        rows, width = x.shape
        nchunk = width // LANES
        x2 = x * x
        p0 = x2.astype(BF16)
        r1 = x2 - p0.astype(F32)
        p1 = r1.astype(BF16)
        p2 = (r1 - p1.astype(F32)).astype(BF16)
        pieces = [p[:, c * LANES:(c + 1) * LANES] for p in (p0, p1, p2) for c in range(nchunk)]
        sums = _dot(jnp.concatenate(pieces, axis=0), gmat)
        out = []
        for c in range(nchunk):
            ss = (sums[c * rows:(c + 1) * rows] + sums[(nchunk + c) * rows:(nchunk + c + 1) * rows]
                  + sums[(2 * nchunk + c) * rows:(2 * nchunk + c + 1) * rows])
            sl = slice(c * LANES, (c + 1) * LANES)
            out.append(x[:, sl] * lax.rsqrt(ss * (1.0 / HD) + EPS) * gain[:, sl])
        return out

    qn = [c.astype(BF16) for c in group_rms_chunks(q_ref[...].astype(F32), qw_ref[...] * (HD ** -0.5 * LOG2_E))]
    kn = group_rms_chunks(jnp.concatenate([kp_ref[...], kc_ref[...]], axis=0).astype(F32), kw_ref[...])
    vf = jnp.concatenate([vp_ref[...], vc_ref[...]], axis=0).astype(F32)
    lo_half = lax.broadcasted_iota(jnp.int32, (2 * Q, LANES), 1) < HD

    for g in range(SW_KV_HEADS):
        kc = kn[g // 2]
        vc = vf[:, (g // 2) * LANES:(g // 2 + 1) * LANES]
        kr = pltpu.roll(kc, HD, axis=1)
        vr = pltpu.roll(vc, HD, axis=1)
        k_lo, k_hi, v_lo, v_hi = (kc, kr, vc, vr) if g % 2 == 0 else (kr, kc, vr, vc)
        ka = jnp.where(lo_half, k_lo, 0.0).astype(BF16)
        kb = jnp.where(lo_half, 0.0, k_hi).astype(BF16)
        vv = jnp.concatenate([jnp.where(lo_half, v_lo, 0.0), jnp.where(lo_half, 0.0, v_hi)], axis=0).astype(BF16)
        nch = SW_GROUP // 2
        lhs = jnp.concatenate(qn[g * nch:(g + 1) * nch], axis=0)
        scores = (_dot_nt(lhs, ka), _dot_nt(lhs, kb))
        probs = ([], [])
        for m in range(nch):
            for par in range(2):
                h = g * SW_GROUP + 2 * m + par
                sink = sink_ref[h] * LOG2_E
                sc = scores[par][m * Q:(m + 1) * Q, :] + bias_ref[h]
                mx = jnp.maximum(jnp.max(sc, axis=-1, keepdims=True), sink)
                e = jnp.exp2(sc - mx)
                denom = jnp.sum(e, axis=-1, keepdims=True) + jnp.exp2(sink - mx)
                probs[par].append((e / denom).astype(BF16))
        p_all = jnp.concatenate([jnp.concatenate(probs[0], axis=0), jnp.concatenate(probs[1], axis=0)], axis=-1)
        o_all = _dot(p_all, vv)
        for m in range(nch):
            c0 = (g * nch + m) * LANES
            out_ref[:, c0:c0 + LANES] = o_all[m * Q:(m + 1) * Q, :].astype(out_ref.dtype)


def _swa(proj, sinks, q_norm_w, k_norm_w, cast_src, B, S, q_off, k_off, v_off):
    T = B * S
    Q = WINDOW
    nb = S // Q
    cast_rows = cast_src.shape[0] // (B * nb)
    assert cast_rows * B * nb == cast_src.shape[0]
    cast_spec = pl.BlockSpec((cast_rows, cast_src.shape[1]), lambda b, n, s: (b * nb + n, 0))
    q_w = SW_HEADS * SW_HD
    kv_w = SW_KV_HEADS * SW_HD
    cur = lambda b, n: b * nb + n
    prev = lambda b, n: b * nb + jnp.maximum(n - 1, 0)
    grid_spec = pltpu.PrefetchScalarGridSpec(
        num_scalar_prefetch=1,
        grid=(B, nb),
        in_specs=[pl.BlockSpec((Q, q_w), lambda b, n, s: (cur(b, n), q_off // q_w)),
                  pl.BlockSpec((Q, kv_w), lambda b, n, s: (prev(b, n), k_off // kv_w)),
                  pl.BlockSpec((Q, kv_w), lambda b, n, s: (cur(b, n), k_off // kv_w)),
                  pl.BlockSpec((Q, kv_w), lambda b, n, s: (prev(b, n), v_off // kv_w)),
                  pl.BlockSpec((Q, kv_w), lambda b, n, s: (cur(b, n), v_off // kv_w)),
                  pl.BlockSpec((1, q_w), lambda b, n, s: (0, 0)),
                  pl.BlockSpec((1, kv_w), lambda b, n, s: (0, 0)),
                  pl.BlockSpec((None, SW_HEADS, Q, 2 * Q), lambda b, n, s: (jnp.minimum(n, 1), 0, 0, 0)),
                  cast_spec],
        out_specs=[pl.BlockSpec((Q, q_w), lambda b, n, s: (cur(b, n), 0)), cast_spec],
    )
    q_gain = jnp.tile(q_norm_w.reshape(1, SW_HD), (1, SW_HEADS))
    k_gain = jnp.tile(k_norm_w.reshape(1, SW_HD), (1, SW_KV_HEADS))
    qi = jnp.arange(Q, dtype=jnp.int32)[:, None]
    kj = jnp.arange(2 * Q, dtype=jnp.int32)[None, :]
    dist = qi + Q - kj
    in_window = (dist >= 0) & (dist < WINDOW)
    slopes = 2.0 ** (-8.0 * jnp.arange(1, SW_HEADS + 1, dtype=F32) / SW_HEADS)
    alibi = -(slopes * LOG2_E)[:, None, None] * dist.astype(F32)[None]
    bias = jnp.stack([jnp.where(in_window & (kj >= Q), alibi, -jnp.inf), jnp.where(in_window, alibi, -jnp.inf)])
    return pl.pallas_call(
        _swa_kernel,
        grid_spec=grid_spec,
        out_shape=[jax.ShapeDtypeStruct((T, q_w), BF16), jax.ShapeDtypeStruct(cast_src.shape, BF16)],
        compiler_params=_cparams("arbitrary", "arbitrary"),
        name="swa",
    )(sinks, proj, proj, proj, proj, proj, q_gain, k_gain, bias, cast_src)


def _merge_kernel(a_ref, b_ref, wa_ref, wb_ref, ga_ref, gb_ref, o_ref):
    ya = _dot(a_ref[...], wa_ref[...])
    yb = _dot(b_ref[...], wb_ref[...])
    mixed = _sigmoid(ga_ref[...].astype(F32)) * ya + _sigmoid(gb_ref[...].astype(F32)) * yb
    o_ref[...] = mixed.astype(o_ref.dtype)


def _merge(ha, hb, wa, wb, proj, ga_off, gb_off):
    T, D = ha.shape
    N = wa.shape[1]
    tm = min(1024, T)
    tn = 1024
    return pl.pallas_call(
        _merge_kernel,
        grid=(T // tm, N // tn),
        in_specs=[pl.BlockSpec((tm, D), lambda i, j: (i, 0)),
                  pl.BlockSpec((tm, D), lambda i, j: (i, 0)),
                  pl.BlockSpec((D, tn), lambda i, j: (0, j)),
                  pl.BlockSpec((D, tn), lambda i, j: (0, j)),
                  pl.BlockSpec((tm, tn), lambda i, j: (i, ga_off // tn + j)),
                  pl.BlockSpec((tm, tn), lambda i, j: (i, gb_off // tn + j))],
        out_specs=pl.BlockSpec((tm, tn), lambda i, j: (i, j)),
        out_shape=jax.ShapeDtypeStruct((T, N), BF16),
        compiler_params=_cparams("arbitrary", "arbitrary"),
        name="merge",
    )(ha, hb, wa, wb, proj, proj)


def _out_proj_kernel(mix_ref, x_ref, wo_ref, nw_ref, wrh_ref, wrl_ref, br_ref, h_ref, xn_ref, idx_ref, gate_ref):
    h = x_ref[...] + _dot(mix_ref[...], wo_ref[...])
    h_ref[...] = h
    xn = _rms(h, nw_ref[...])
    xn_ref[...] = xn
    xn_hi = xn.astype(BF16)
    xn_lo = (xn - xn_hi.astype(F32)).astype(BF16)
    logits = (_dot(xn_hi, wrh_ref[...]) + (_dot(xn_hi, wrl_ref[...]) + _dot(xn_lo, wrh_ref[...]))
              + br_ref[...])
    tm = logits.shape[0]
    lane_i = lax.broadcasted_iota(jnp.int32, (tm, LANES), 1)
    lane = lane_i.astype(F32)
    vals, idxs = [], []
    for _ in range(TOP_K):
        mx = jnp.max(logits, axis=-1, keepdims=True)
        ix = jnp.min(jnp.where(logits == mx, lane, float(LANES)), axis=-1, keepdims=True)
        vals.append(mx)
        idxs.append(ix)
        logits = jnp.where(lane == ix, -jnp.inf, logits)
    es = [jnp.exp(v - vals[0]) for v in vals]
    tot = es[0]
    for e in es[1:]:
        tot = tot + e
    idx_out = jnp.zeros((tm, LANES), F32)
    gate_out = jnp.zeros((tm, LANES), F32)
    for k in range(TOP_K):
        idx_out = jnp.where(lane_i == k, idxs[k], idx_out)
        gate_out = jnp.where(lane_i == k, es[k] / tot, gate_out)
    idx_ref[...] = idx_out.astype(jnp.int32)
    gate_ref[...] = gate_out


def _out_proj(mixed, x2d, w_out, norm_w, w_router, b_router):
    T, D = x2d.shape
    tm = min(512, T)
    row = lambda i: (i, 0)
    const = lambda i: (0, 0)
    wr_hi = w_router.astype(BF16)
    wr_lo = (w_router - wr_hi.astype(F32)).astype(BF16)
    return pl.pallas_call(
        _out_proj_kernel,
        grid=(T // tm,),
        in_specs=[pl.BlockSpec((tm, D), row), pl.BlockSpec((tm, D), row),
                  pl.BlockSpec((D, D), const), pl.BlockSpec((1, D), const),
                  pl.BlockSpec((D, LANES), const), pl.BlockSpec((D, LANES), const),
                  pl.BlockSpec((1, LANES), const)],
        out_specs=[pl.BlockSpec((tm, D), row), pl.BlockSpec((tm, D), row),
                   pl.BlockSpec((tm, LANES), row), pl.BlockSpec((tm, LANES), row)],
        out_shape=[jax.ShapeDtypeStruct((T, D), F32), jax.ShapeDtypeStruct((T, D), F32),
                   jax.ShapeDtypeStruct((T, LANES), jnp.int32), jax.ShapeDtypeStruct((T, LANES), F32)],
        compiler_params=_cparams("arbitrary"),
        name="out_proj",
    )(mixed, x2d, w_out, norm_w, wr_hi, wr_lo, b_router)


def _gather_copy(src_hbm, dst_buf, sem, src_row, dst_row):
    return pltpu.make_async_copy(src_hbm.at[pl.ds(src_row, 1), :], dst_buf.at[pl.ds(dst_row, 1), :], sem)


def _expert_kernel(be_ref, nh_ref, nv_ref, tok_ref, tokn_ref, x_hbm, wg_ref, wl_ref, bg_ref, bl_ref, wd_ref, bd_ref,
                   o_ref, xbuf, xb, act, sem):
    i = pl.program_id(0)
    c = pl.program_id(1)
    n_valid = nv_ref[0]
    halves = nh_ref[i]
    rows = xb.shape[0]
    half = rows // MOE_HALVES
    nu = act.shape[0]
    nd = xb.shape[1] // o_ref.shape[1]
    fc = act.shape[2]

    @pl.when((c == 0) & (i == 0))
    def _():
        def body(r, carry):
            _gather_copy(x_hbm, xbuf, sem, tok_ref[0, 0, r], r).start()
            return carry
        lax.fori_loop(0, rows, body, 0, unroll=8)

    @pl.when((c == 0) & (i <= n_valid))
    def _():
        pltpu.make_async_copy(x_hbm.at[pl.ds(0, rows), :], xbuf, sem).wait()

    @pl.when((c == 0) & (i < n_valid))
    def _():
        xb[...] = xbuf[...].astype(BF16)

    def request_next_rows(first, count):
        dst = xbuf.at[pl.ds(pl.multiple_of(first, 8), count), :]
        for r in range(count):
            _gather_copy(x_hbm, dst, sem, tokn_ref[0, 0, first + r], r).start()

    per_up = rows // nu
    assert nu * per_up == rows and per_up % 8 == 0

    def up(hf):
        rs = slice(hf * half, (hf + 1) * half)
        x = xb[rs, :]
        h_glu = jnp.minimum(_dot(x, wg_ref[...]) + bg_ref[...], SWIGLU_LIMIT)
        h_lin = jnp.clip(_dot(x, wl_ref[...]) + bl_ref[...], -SWIGLU_LIMIT, SWIGLU_LIMIT)
        act[c, rs, :] = (h_glu * _sigmoid(SWIGLU_ALPHA * h_glu) * (h_lin + 1.0)).astype(BF16)

    def down(hf):
        rs = slice(hf * half, (hf + 1) * half)
        y = _dot(act[0, rs, :], wd_ref[0:fc, :])
        for u in range(1, nu):
            y = y + _dot(act[u, rs, :], wd_ref[u * fc:(u + 1) * fc, :])
        o_ref[rs, :] = y + bd_ref[...]

    @pl.when((halves > 0) & (c < nu))
    def _():
        request_next_rows(c * per_up, per_up)
        up(0)

    @pl.when((halves > 0) & (c >= nu))
    def _():
        down(0)

    for hf in range(1, MOE_HALVES):
        @pl.when((halves > hf) & (c < nu))
        def _():
            up(hf)

        @pl.when((halves > hf) & (c >= nu))
        def _():
            down(hf)

        @pl.when((halves <= hf) & (c >= nu))
        def _():
            rs = slice(hf * half, (hf + 1) * half)
            o_ref[rs, :] = jnp.zeros((half, o_ref.shape[1]), o_ref.dtype)

    @pl.when((halves == 0) & (c >= nu))
    def _():
        o_ref[0:half, :] = jnp.zeros((half, o_ref.shape[1]), o_ref.dtype)


def _experts(xp, buf_tok, block_e, block_halves, n_valid, w_up, b_up, w_down, b_down):
    E, D, F2 = w_up.shape
    F = F2 // 2
    rows = MOE_ROWS * MOE_HALVES
    fc = MOE_FF_CHUNK
    dn = MOE_OUT_CHUNK
    nu = F // fc
    nd = D // dn
    nblk = buf_tok.shape[0] // rows
    tok3 = buf_tok.reshape(nblk, 1, rows)
    cu = lambda i, c, nv: jnp.where(i < nv[0], jnp.minimum(c, nu - 1), nu - 1)
    cd = lambda i, c, nv: jnp.where(i < nv[0], jnp.maximum(c - nu, 0), nd - 1)
    co = lambda c: jnp.maximum(c - nu, 0)
    grid_spec = pltpu.PrefetchScalarGridSpec(
        num_scalar_prefetch=3,
        grid=(nblk, nu + nd),
        in_specs=[
            pl.BlockSpec((1, 1, rows), lambda i, c, be, nh, nv: (i, 0, 0), memory_space=pltpu.SMEM),
            pl.BlockSpec((1, 1, rows), lambda i, c, be, nh, nv: (jnp.minimum(i + 1, nblk - 1), 0, 0),
                         memory_space=pltpu.SMEM),
            pl.BlockSpec(memory_space=pl.ANY),
            pl.BlockSpec((None, D, fc), lambda i, c, be, nh, nv: (be[i], 0, cu(i, c, nv))),
            pl.BlockSpec((None, D, fc), lambda i, c, be, nh, nv: (be[i], 0, nu + cu(i, c, nv))),
            pl.BlockSpec((None, 1, fc), lambda i, c, be, nh, nv: (be[i], 0, cu(i, c, nv))),
            pl.BlockSpec((None, 1, fc), lambda i, c, be, nh, nv: (be[i], 0, nu + cu(i, c, nv))),
            pl.BlockSpec((None, F, dn), lambda i, c, be, nh, nv: (be[i], 0, cd(i, c, nv))),
            pl.BlockSpec((None, 1, dn), lambda i, c, be, nh, nv: (be[i], 0, cd(i, c, nv))),
        ],
        out_specs=pl.BlockSpec((rows, dn), lambda i, c, be, nh, nv: (i, co(c))),
        scratch_shapes=[pltpu.VMEM((rows, D), F32), pltpu.VMEM((rows, D), BF16),
                        pltpu.VMEM((nu, rows, fc), BF16), pltpu.SemaphoreType.DMA],
    )
    return pl.pallas_call(
        _expert_kernel,
        grid_spec=grid_spec,
        out_shape=jax.ShapeDtypeStruct((nblk * rows, D), F32),
        compiler_params=_cparams("arbitrary", "arbitrary"),
        name="experts",
    )(block_e, block_halves, n_valid, tok3, tok3, xp, w_up, w_up, b_up, b_up, w_down, b_down)


def _combine_kernel(slot_ref, slotn_ref, y_hbm, h_ref, gate_ref, p_ref, nw_ref, wg_ref, wp_ref, o_ref, ybuf, sem):
    i = pl.program_id(0)
    nsteps = pl.num_programs(0)
    tm = h_ref.shape[0]
    n = TOP_K * tm
    s = i % 2

    def wait_rows(b):
        pltpu.make_async_copy(y_hbm.at[pl.ds(0, n), :], ybuf.at[b], sem.at[b]).wait()

    @pl.when(i == 0)
    def _():
        def body(r, carry):
            _gather_copy(y_hbm, ybuf.at[0], sem.at[0], slot_ref[0, 0, r], r).start()
            return carry
        lax.fori_loop(0, n, body, 0, unroll=8)

    wait_rows(s)
    nxt = ybuf.at[1 - s]
    for r in range(n):
        _gather_copy(y_hbm, nxt, sem.at[1 - s], slotn_ref[0, 0, r], r).start()

    h = h_ref[...]
    gates = gate_ref[...]
    for k in range(TOP_K):
        h = h + gates[:, k:k + 1] * ybuf[s, k * tm:(k + 1) * tm, :]
    gate = _sigmoid(_dot(_rms(h, nw_ref[...]).astype(BF16), wg_ref[...]))
    o_ref[...] = h + gate * _dot(p_ref[...].astype(BF16), wp_ref[...])

    @pl.when(i == nsteps - 1)
    def _():
        wait_rows(1 - s)


def _combine(ys, slots, h1, gates, p2d, norm_w, w_gate, w_proj):
    T, D = h1.shape
    PD = p2d.shape[1]
    tm = min(256, T)
    nsteps = T // tm
    slot3 = slots.reshape(nsteps, tm, TOP_K).transpose(0, 2, 1).reshape(nsteps, 1, TOP_K * tm)
    row = lambda i: (i, 0)
    const = lambda i: (0, 0)
    return pl.pallas_call(
        _combine_kernel,
        grid=(nsteps,),
        in_specs=[pl.BlockSpec((1, 1, TOP_K * tm), lambda i: (i, 0, 0), memory_space=pltpu.SMEM),
                  pl.BlockSpec((1, 1, TOP_K * tm), lambda i: (jnp.minimum(i + 1, nsteps - 1), 0, 0),
                               memory_space=pltpu.SMEM),
                  pl.BlockSpec(memory_space=pl.ANY),
                  pl.BlockSpec((tm, D), row), pl.BlockSpec((tm, LANES), row), pl.BlockSpec((tm, PD), row),
                  pl.BlockSpec((1, D), const), pl.BlockSpec((D, D), const), pl.BlockSpec((PD, D), const)],
        out_specs=pl.BlockSpec((tm, D), row),
        out_shape=jax.ShapeDtypeStruct((T, D), F32),
        scratch_shapes=[pltpu.VMEM((2, TOP_K * tm, D), F32), pltpu.SemaphoreType.DMA((2,))],
        compiler_params=_cparams("arbitrary"),
        name="combine",
    )(slot3, slot3, ys, h1, gates, p2d, norm_w, w_gate, w_proj)


def _route_rank_kernel(idx_ref, rank_ref, cnt_ref, base_ref):
    @pl.when(pl.program_id(0) == 0)
    def _():
        base_ref[...] = jnp.zeros_like(base_ref)

    tm = idx_ref.shape[0]
    lane = lax.broadcasted_iota(jnp.int32, (tm, LANES), 1)
    idx = idx_ref[...]
    onehots = [(lane == idx[:, k:k + 1]).astype(F32) for k in range(TOP_K)]
    anyhot = onehots[0]
    for oh in onehots[1:]:
        anyhot = anyhot + oh
    r = lax.broadcasted_iota(jnp.int32, (tm, tm), 0)
    c = lax.broadcasted_iota(jnp.int32, (tm, tm), 1)
    before = (c < r).astype(BF16)
    prior = _dot(before, anyhot.astype(BF16)) + base_ref[...]
    rank = jnp.zeros((tm, LANES), F32)
    for k in range(TOP_K):
        rank = jnp.where(lane == k, jnp.sum(onehots[k] * prior, axis=-1, keepdims=True), rank)
    rank_ref[...] = rank.astype(jnp.int32)
    base_ref[...] += jnp.sum(anyhot, axis=0, keepdims=True)
    cnt_ref[...] = base_ref[...]


def _route_rank(idx_pad):
    T = idx_pad.shape[0]
    tm = min(ROUTE_ROWS, T)
    return pl.pallas_call(
        _route_rank_kernel,
        grid=(T // tm,),
        in_specs=[pl.BlockSpec((tm, LANES), lambda i: (i, 0))],
        out_specs=[pl.BlockSpec((tm, LANES), lambda i: (i, 0)), pl.BlockSpec((1, LANES), lambda i: (0, 0))],
        out_shape=[jax.ShapeDtypeStruct((T, LANES), jnp.int32), jax.ShapeDtypeStruct((1, LANES), F32)],
        scratch_shapes=[pltpu.VMEM((1, LANES), F32)],
        compiler_params=_cparams("arbitrary"),
        name="route_rank",
    )(idx_pad)


def _routing(idx_pad):
    T = idx_pad.shape[0]
    K = TOP_K
    N = T * K
    E = N_EXPERTS
    rows = MOE_ROWS * MOE_HALVES
    rank_pad, counts_pad = _route_rank(idx_pad)
    counts = counts_pad[0, :E].astype(jnp.int32)
    starts = jnp.cumsum(counts) - counts
    padded = (counts + rows - 1) // rows * rows
    pad_ends = jnp.cumsum(padded)
    pad_starts = pad_ends - padded
    top_idx = idx_pad[:, :K]
    expert_ids = jnp.arange(E, dtype=jnp.int32)
    slots = jnp.sum(jnp.where(top_idx[:, :, None] == expert_ids, pad_starts, 0), axis=-1) + rank_pad[:, :K]
    order = jnp.argsort(top_idx.reshape(-1), stable=True).astype(jnp.int32)
    P = N + E * rows
    nblk = P // rows
    blk_start = jnp.arange(nblk, dtype=jnp.int32) * rows
    block_e = jnp.minimum(jnp.sum((pad_ends[None, :] <= blk_start[:, None]).astype(jnp.int32), axis=1), E - 1)
    blk_off = blk_start - jnp.take(pad_starts, block_e)
    blk_cnt = jnp.take(counts, block_e)
    r_slot = blk_off[:, None] + jnp.arange(rows, dtype=jnp.int32)[None, :]
    src = jnp.clip(jnp.take(starts, block_e)[:, None] + r_slot, 0, N - 1)
    buf_tok = jnp.where(r_slot < blk_cnt[:, None], jnp.take(order, src.reshape(-1)).reshape(nblk, rows) // K, 0)
    buf_tok = buf_tok.reshape(-1)
    n_valid = (pad_ends[-1] // rows).astype(jnp.int32).reshape(1)
    remaining = blk_cnt - blk_off
    block_halves = jnp.clip((remaining + MOE_ROWS - 1) // MOE_ROWS, 0, MOE_HALVES).astype(jnp.int32)
    return slots, buf_tok, block_e, block_halves, n_valid


def _layer(h2d, p2d, B, S, norm_mix_w, w_in, ml_igate_b, ml_fgate_b, ml_norm_w, sw_q_norm_w, sw_k_norm_w, sw_sinks,
           w_branch_a, w_branch_b, w_out, norm_ffn_w, w_router, b_router, w_expert_up, b_expert_up,
           w_expert_down, b_expert_down, norm_ple_w, w_ple_gate, w_ple_proj):
    T, D = h2d.shape
    H = ML_HEADS
    qk_w, v_w = H * ML_DQK, H * ML_DV
    sq_w, skv_w = SW_HEADS * SW_HD, SW_KV_HEADS * SW_HD
    g0 = 2 * qk_w + 2 * v_w
    s0 = g0 + 2 * H
    w_sq = w_in[:, s0:s0 + sq_w]
    w_skv = w_in[:, s0 + sq_w:s0 + sq_w + 2 * skv_w]
    w_gab = w_in[:, s0 + sq_w + 2 * skv_w:]
    w_main = jnp.concatenate([w_in[:, :g0], w_sq, w_gab, w_skv], axis=1).astype(BF16)
    w_gate = jnp.pad(w_in[:, g0:s0], ((0, 0), (0, LANES - 2 * H))).astype(BF16)
    sq_off = g0
    ga_off = sq_off + sq_w
    gb_off = ga_off + D
    sk_off = gb_off + D
    sv_off = sk_off + skv_w

    proj, gates = _in_proj(h2d, norm_mix_w.reshape(1, D), w_main, w_gate)

    gate_b = jnp.concatenate([ml_igate_b, ml_fgate_b]).astype(F32)
    bias_row = jnp.pad(gate_b, (0, LANES - 2 * H)).reshape(1, LANES)
    bias_col = gate_b.reshape(2 * H, 1)
    gates_t = gates[:, :2 * H].T
    E = N_EXPERTS
    up_shape, down_shape = w_expert_up.shape, w_expert_down.shape
    h_ml, w_up_bf = _mlstm(proj, gates, gates_t, bias_row, bias_col, ml_norm_w.reshape(1, v_w),
                           w_expert_up.reshape(E * up_shape[1], up_shape[2]), B, S)

    h_sw, w_down_bf = _swa(proj, sw_sinks.astype(F32), sw_q_norm_w.reshape(1, SW_HD), sw_k_norm_w.reshape(1, SW_HD),
                           w_expert_down.reshape(E * down_shape[1], down_shape[2]), B, S, sq_off, sk_off, sv_off)

    mixed = _merge(h_ml, h_sw, w_branch_a.astype(BF16), w_branch_b.astype(BF16), proj, ga_off, gb_off)

    wr = jnp.pad(w_router, ((0, 0), (0, LANES - N_EXPERTS)))
    br = jnp.pad(b_router, (0, LANES - N_EXPERTS), constant_values=NEG_BIG).reshape(1, LANES)
    h1, xn, idx_pad, gate_pad = _out_proj(mixed, h2d, w_out.astype(BF16), norm_ffn_w.reshape(1, D), wr, br)

    slots, buf_tok, block_e, block_halves, n_valid = _routing(idx_pad)
    ys = _experts(xn, buf_tok, block_e, block_halves, n_valid,
                  w_up_bf.reshape(up_shape), b_expert_up.reshape(E, 1, -1),
                  w_down_bf.reshape(down_shape), b_expert_down.reshape(E, 1, -1))

    return _combine(ys, slots, h1, gate_pad, p2d, norm_ple_w.reshape(1, D),
                    w_ple_gate.astype(BF16), w_ple_proj.astype(BF16))


def kernel(x, p, norm_mix_w, w_in, ml_igate_b, ml_fgate_b, ml_norm_w, sw_q_norm_w, sw_k_norm_w, sw_sinks,
           w_branch_a, w_branch_b, w_out, norm_ffn_w, w_router, b_router, w_expert_up, b_expert_up,
           w_expert_down, b_expert_down, norm_ple_w, w_ple_gate, w_ple_proj):
    B, S, D = x.shape
    depth = p.shape[0]
    h = x.reshape(B * S, D)
    for i in range(depth):
        h = _layer(h, p[i].reshape(B * S, -1), B, S, norm_mix_w[i], w_in[i], ml_igate_b[i], ml_fgate_b[i],
                   ml_norm_w[i], sw_q_norm_w[i], sw_k_norm_w[i], sw_sinks[i], w_branch_a[i], w_branch_b[i],
                   w_out[i], norm_ffn_w[i], w_router[i], b_router[i], w_expert_up[i], b_expert_up[i],
                   w_expert_down[i], b_expert_down[i], norm_ple_w[i], w_ple_gate[i], w_ple_proj[i])
    return h.reshape(B, S, D)
```

```python
import functools

import jax
import jax.numpy as jnp
from jax import lax
from jax.experimental import pallas as pl
from jax.experimental.pallas import tpu as pltpu

F32 = jnp.float32
BF16 = jnp.bfloat16

EPS = 1e-6
ML_HEADS = 8
ML_DQK = 128
ML_DV = 256
GATE_CAP = 15.0
SW_HEADS = 32
SW_KV_HEADS = 4
SW_GROUP = SW_HEADS // SW_KV_HEADS
SW_HD = 64
WINDOW = 128
N_EXPERTS = 32
TOP_K = 4
SWIGLU_LIMIT = 7.0
SWIGLU_ALPHA = 1.702

LANES = 128
VMEM_LIMIT = 56 * 1024 * 1024

ML_CHUNK = 256
MOE_ROWS = 512
MOE_HALVES = 2
MOE_FF_CHUNK = 512
MOE_OUT_CHUNK = 1024
ROUTE_ROWS = 512
NEG_BIG = -1e30
LOG2_E = 1.4426950408889634


def _cparams(*sem):
    return pltpu.CompilerParams(dimension_semantics=sem, vmem_limit_bytes=VMEM_LIMIT)


def _dot(a, b):
    return jnp.dot(a, b, preferred_element_type=F32)


def _dot_nt(a, b):
    return lax.dot_general(a, b, (((1,), (1,)), ((), ())), preferred_element_type=F32)


def _dot_tn(a, b):
    return lax.dot_general(a, b, (((0,), (0,)), ((), ())), preferred_element_type=F32)


def _dot_f32(a, b):
    return jnp.dot(a, b, preferred_element_type=F32, precision=lax.Precision.HIGHEST)


def _rms(x, w):
    return x * lax.rsqrt(jnp.mean(x * x, axis=-1, keepdims=True) + EPS) * w


def _sigmoid(x):
    return 1.0 / (1.0 + jnp.exp(-x))


def _in_proj_kernel(x_ref, nw_ref, w_ref, wg_ref, o_ref, g_ref, xn_ref):
    @pl.when(pl.program_id(1) == 0)
    def _():
        xn = _rms(x_ref[...], nw_ref[...]).astype(BF16)
        xn_ref[...] = xn
        g_ref[...] = _dot(xn, wg_ref[...])

    o_ref[...] = _dot(xn_ref[...], w_ref[...]).astype(o_ref.dtype)


def _in_proj(x2d, norm_w, w_main, w_gate):
    T, D = x2d.shape
    N = w_main.shape[1]
    tm = min(1024, T)
    tn = 1280
    return pl.pallas_call(
        _in_proj_kernel,
        grid=(T // tm, N // tn),
        in_specs=[pl.BlockSpec((tm, D), lambda i, j: (i, 0)),
                  pl.BlockSpec((1, D), lambda i, j: (0, 0)),
                  pl.BlockSpec((D, tn), lambda i, j: (0, j)),
                  pl.BlockSpec((D, LANES), lambda i, j: (0, 0))],
        out_specs=[pl.BlockSpec((tm, tn), lambda i, j: (i, j)),
                   pl.BlockSpec((tm, LANES), lambda i, j: (i, 0))],
        out_shape=[jax.ShapeDtypeStruct((T, N), BF16), jax.ShapeDtypeStruct((T, LANES), F32)],
        scratch_shapes=[pltpu.VMEM((tm, D), BF16)],
        compiler_params=_cparams("arbitrary", "arbitrary"),
        name="in_proj",
    )(x2d, norm_w, w_main, w_gate)


def _log_sigmoid(z):
    return jnp.minimum(z, 0.0) - jnp.log1p(jnp.exp(-jnp.abs(z)))


def _soft_cap(z):
    return GATE_CAP * jnp.tanh(z / GATE_CAP)


def _mlstm_kernel(q_ref, k_ref, v_ref, o_ref, g_ref, gt_ref, brow_ref, bcol_ref, nw_ref, wsrc_ref,
                  out_ref, wdst_ref, c_ref, n_ref, m_ref):
    wdst_ref[...] = wsrc_ref[...].astype(wdst_ref.dtype)

    L = q_ref.shape[0]
    H = ML_HEADS

    @pl.when(pl.program_id(1) == 0)
    def _():
        c_ref[...] = jnp.zeros_like(c_ref)
        n_ref[...] = jnp.zeros_like(n_ref)
        m_ref[...] = jnp.zeros_like(m_ref)

    g = g_ref[...] + brow_ref[...]
    gt = gt_ref[...] + bcol_ref[...]
    li_col = _soft_cap(g)
    lf_col = _log_sigmoid(_soft_cap(g))
    li_row = _soft_cap(gt[0:H, :])
    lf_row = _log_sigmoid(_soft_cap(gt[H:2 * H, :]))

    row = lax.broadcasted_iota(jnp.int32, (L, L), 0)
    col = lax.broadcasted_iota(jnp.int32, (L, L), 1)
    causal = col <= row
    tri = causal.astype(F32)
    tri_t = (row <= col).astype(F32)
    b_col = _dot_f32(tri, lf_col)
    b_row = _dot_f32(lf_row, tri_t)
    a_col = li_col[:, 0:H] - b_col[:, H:2 * H]
    a_row = li_row - b_row

    scale = ML_DQK ** -0.5
    for h in range(H):
        qh = q_ref[:, h * ML_DQK:(h + 1) * ML_DQK]
        kh = k_ref[:, h * ML_DQK:(h + 1) * ML_DQK]
        vh = v_ref[:, h * ML_DV:(h + 1) * ML_DV]
        m_prev = m_ref[h:h + 1, 0:1]
        c_prev = c_ref[h]
        n_prev = n_ref[h:h + 1, :]

        amat = jnp.where(causal, a_row[h:h + 1, :], -jnp.inf)
        big_m = jnp.maximum(jnp.max(amat, axis=-1, keepdims=True), m_prev)
        dmat = jnp.exp(amat - big_m)
        inter = jnp.exp(m_prev - big_m)

        s = _dot_nt(qh, kh) * dmat
        qf = qh.astype(F32)
        num = _dot(s.astype(BF16), vh) + inter * _dot(qh, c_prev.astype(BF16))
        den = jnp.sum(s, axis=-1, keepdims=True) + inter * jnp.sum(qf * n_prev, axis=-1, keepdims=True)
        floor = jnp.exp(-(b_col[:, H + h:H + h + 1] + big_m))
        cell = (scale * num) / jnp.maximum(scale * jnp.abs(den), floor)

        w_head = nw_ref[:, h * ML_DV:(h + 1) * ML_DV]
        og = _sigmoid(o_ref[:, h * ML_DV:(h + 1) * ML_DV].astype(F32))
        out_ref[:, h * ML_DV:(h + 1) * ML_DV] = (_rms(cell, w_head) * og).astype(out_ref.dtype)

        m_last = big_m[L - 1:L, :]
        w_col = jnp.exp(a_col[:, h:h + 1] - m_last)
        decay = jnp.exp(m_prev - m_last)
        kw = kh.astype(F32) * w_col
        c_ref[h] = decay * c_prev + _dot_tn(kw.astype(BF16), vh)
        n_ref[h:h + 1, :] = decay * n_prev + jnp.sum(kw, axis=0, keepdims=True)
        m_new = b_col[L - 1:L, H + h:H + h + 1] + m_last
        m_ref[h:h + 1, :] = jnp.broadcast_to(m_new, (1, LANES))


def _mlstm(proj, gates, gates_t, bias_row, bias_col, norm_w, cast_src, B, S):
    T = B * S
    L = min(ML_CHUNK, S)
    nc = S // L
    cast_rows = cast_src.shape[0] // (B * nc)
    assert cast_rows * B * nc == cast_src.shape[0]
    cast_spec = pl.BlockSpec((cast_rows, cast_src.shape[1]), lambda b, c: (b * nc + c, 0))
    qk_w = ML_HEADS * ML_DQK
    v_w = ML_HEADS * ML_DV
    assert 2 * qk_w == v_w
    rows = lambda b, c: b * nc + c
    return pl.pallas_call(
        _mlstm_kernel,
        grid=(B, nc),
        in_specs=[pl.BlockSpec((L, qk_w), lambda b, c: (rows(b, c), 0)),
                  pl.BlockSpec((L, qk_w), lambda b, c: (rows(b, c), 1)),
                  pl.BlockSpec((L, v_w), lambda b, c: (rows(b, c), 1)),
                  pl.BlockSpec((L, v_w), lambda b, c: (rows(b, c), 2)),
                  pl.BlockSpec((L, LANES), lambda b, c: (rows(b, c), 0)),
                  pl.BlockSpec((2 * ML_HEADS, L), lambda b, c: (0, rows(b, c))),
                  pl.BlockSpec((1, LANES), lambda b, c: (0, 0)),
                  pl.BlockSpec((2 * ML_HEADS, 1), lambda b, c: (0, 0)),
                  pl.BlockSpec((1, v_w), lambda b, c: (0, 0)),
                  cast_spec],
        out_specs=[pl.BlockSpec((L, v_w), lambda b, c: (rows(b, c), 0)), cast_spec],
        out_shape=[jax.ShapeDtypeStruct((T, v_w), BF16), jax.ShapeDtypeStruct(cast_src.shape, BF16)],
        scratch_shapes=[pltpu.VMEM((ML_HEADS, ML_DQK, ML_DV), F32),
                        pltpu.VMEM((ML_HEADS, ML_DQK), F32),
                        pltpu.VMEM((ML_HEADS, LANES), F32)],
        compiler_params=_cparams("arbitrary", "arbitrary"),
        name="mlstm",
    )(proj, proj, proj, proj, gates, gates_t, bias_row, bias_col, norm_w, cast_src)


def _swa_kernel(sink_ref, q_ref, kp_ref, kc_ref, vp_ref, vc_ref, qw_ref, kw_ref, bias_ref, wsrc_ref,
                out_ref, wdst_ref):
    wdst_ref[...] = wsrc_ref[...].astype(wdst_ref.dtype)
    Q = q_ref.shape[0]
    HD = SW_HD
    assert LANES == 2 * HD

    gr = lax.broadcasted_iota(jnp.int32, (LANES, LANES), 0) // HD
    gc = lax.broadcasted_iota(jnp.int32, (LANES, LANES), 1) // HD
    gmat = (gr == gc).astype(BF16)

    def group_rms_chunks(x, gain):
        rows, width = x.shape
        nchunk = width // LANES
        x2 = x * x
        p0 = x2.astype(BF16)
        r1 = x2 - p0.astype(F32)
        p1 = r1.astype(BF16)
        p2 = (r1 - p1.astype(F32)).astype(BF16)
        pieces = [p[:, c * LANES:(c + 1) * LANES] for p in (p0, p1, p2) for c in range(nchunk)]
        sums = _dot(jnp.concatenate(pieces, axis=0), gmat)
        out = []
        for c in range(nchunk):
            ss = (sums[c * rows:(c + 1) * rows] + sums[(nchunk + c) * rows:(nchunk + c + 1) * rows]
                  + sums[(2 * nchunk + c) * rows:(2 * nchunk + c + 1) * rows])
            sl = slice(c * LANES, (c + 1) * LANES)
            out.append(x[:, sl] * lax.rsqrt(ss * (1.0 / HD) + EPS) * gain[:, sl])
        return out

    qn = [c.astype(BF16) for c in group_rms_chunks(q_ref[...].astype(F32), qw_ref[...] * (HD ** -0.5 * LOG2_E))]
    kn = group_rms_chunks(jnp.concatenate([kp_ref[...], kc_ref[...]], axis=0).astype(F32), kw_ref[...])
    vf = jnp.concatenate([vp_ref[...], vc_ref[...]], axis=0).astype(F32)
    lo_half = lax.broadcasted_iota(jnp.int32, (2 * Q, LANES), 1) < HD

    for g in range(SW_KV_HEADS):
        kc = kn[g // 2]
        vc = vf[:, (g // 2) * LANES:(g // 2 + 1) * LANES]
        kr = pltpu.roll(kc, HD, axis=1)
        vr = pltpu.roll(vc, HD, axis=1)
        k_lo, k_hi, v_lo, v_hi = (kc, kr, vc, vr) if g % 2 == 0 else (kr, kc, vr, vc)
        ka = jnp.where(lo_half, k_lo, 0.0).astype(BF16)
        kb = jnp.where(lo_half, 0.0, k_hi).astype(BF16)
        vv = jnp.concatenate([jnp.where(lo_half, v_lo, 0.0), jnp.where(lo_half, 0.0, v_hi)], axis=0).astype(BF16)
        nch = SW_GROUP // 2
        lhs = jnp.concatenate(qn[g * nch:(g + 1) * nch], axis=0)
        scores = (_dot_nt(lhs, ka), _dot_nt(lhs, kb))
        probs = ([], [])
        for m in range(nch):
            for par in range(2):
                h = g * SW_GROUP + 2 * m + par
                sink = sink_ref[h] * LOG2_E
                sc = scores[par][m * Q:(m + 1) * Q, :] + bias_ref[h]
                mx = jnp.maximum(jnp.max(sc, axis=-1, keepdims=True), sink)
                e = jnp.exp2(sc - mx)
                denom = jnp.sum(e, axis=-1, keepdims=True) + jnp.exp2(sink - mx)
                probs[par].append((e / denom).astype(BF16))
        p_all = jnp.concatenate([jnp.concatenate(probs[0], axis=0), jnp.concatenate(probs[1], axis=0)], axis=-1)
        o_all = _dot(p_all, vv)
        for m in range(nch):
            c0 = (g * nch + m) * LANES
            out_ref[:, c0:c0 + LANES] = o_all[m * Q:(m + 1) * Q, :].astype(out_ref.dtype)


def _swa(proj, sinks, q_norm_w, k_norm_w, cast_src, B, S, q_off, k_off, v_off):
    T = B * S
    Q = WINDOW
    nb = S // Q
    cast_rows = cast_src.shape[0] // (B * nb)
    assert cast_rows * B * nb == cast_src.shape[0]
    cast_spec = pl.BlockSpec((cast_rows, cast_src.shape[1]), lambda b, n, s: (b * nb + n, 0))
    q_w = SW_HEADS * SW_HD
    kv_w = SW_KV_HEADS * SW_HD
    cur = lambda b, n: b * nb + n
    prev = lambda b, n: b * nb + jnp.maximum(n - 1, 0)
    grid_spec = pltpu.PrefetchScalarGridSpec(
        num_scalar_prefetch=1,
        grid=(B, nb),
        in_specs=[pl.BlockSpec((Q, q_w), lambda b, n, s: (cur(b, n), q_off // q_w)),
                  pl.BlockSpec((Q, kv_w), lambda b, n, s: (prev(b, n), k_off // kv_w)),
                  pl.BlockSpec((Q, kv_w), lambda b, n, s: (cur(b, n), k_off // kv_w)),
                  pl.BlockSpec((Q, kv_w), lambda b, n, s: (prev(b, n), v_off // kv_w)),
                  pl.BlockSpec((Q, kv_w), lambda b, n, s: (cur(b, n), v_off // kv_w)),
                  pl.BlockSpec((1, q_w), lambda b, n, s: (0, 0)),
                  pl.BlockSpec((1, kv_w), lambda b, n, s: (0, 0)),
                  pl.BlockSpec((None, SW_HEADS, Q, 2 * Q), lambda b, n, s: (jnp.minimum(n, 1), 0, 0, 0)),
                  cast_spec],
        out_specs=[pl.BlockSpec((Q, q_w), lambda b, n, s: (cur(b, n), 0)), cast_spec],
    )
    q_gain = jnp.tile(q_norm_w.reshape(1, SW_HD), (1, SW_HEADS))
    k_gain = jnp.tile(k_norm_w.reshape(1, SW_HD), (1, SW_KV_HEADS))
    qi = jnp.arange(Q, dtype=jnp.int32)[:, None]
    kj = jnp.arange(2 * Q, dtype=jnp.int32)[None, :]
    dist = qi + Q - kj
    in_window = (dist >= 0) & (dist < WINDOW)
    slopes = 2.0 ** (-8.0 * jnp.arange(1, SW_HEADS + 1, dtype=F32) / SW_HEADS)
    alibi = -(slopes * LOG2_E)[:, None, None] * dist.astype(F32)[None]
    bias = jnp.stack([jnp.where(in_window & (kj >= Q), alibi, -jnp.inf), jnp.where(in_window, alibi, -jnp.inf)])
    return pl.pallas_call(
        _swa_kernel,
        grid_spec=grid_spec,
        out_shape=[jax.ShapeDtypeStruct((T, q_w), BF16), jax.ShapeDtypeStruct(cast_src.shape, BF16)],
        compiler_params=_cparams("arbitrary", "arbitrary"),
        name="swa",
    )(sinks, proj, proj, proj, proj, proj, q_gain, k_gain, bias, cast_src)


def _merge_kernel(a_ref, b_ref, wa_ref, wb_ref, ga_ref, gb_ref, o_ref):
    ya = _dot(a_ref[...], wa_ref[...])
    yb = _dot(b_ref[...], wb_ref[...])
    mixed = _sigmoid(ga_ref[...].astype(F32)) * ya + _sigmoid(gb_ref[...].astype(F32)) * yb
    o_ref[...] = mixed.astype(o_ref.dtype)


def _merge(ha, hb, wa, wb, proj, ga_off, gb_off):
    T, D = ha.shape
    N = wa.shape[1]
    tm = min(1024, T)
    tn = 1024
    return pl.pallas_call(
        _merge_kernel,
        grid=(T // tm, N // tn),
        in_specs=[pl.BlockSpec((tm, D), lambda i, j: (i, 0)),
                  pl.BlockSpec((tm, D), lambda i, j: (i, 0)),
                  pl.BlockSpec((D, tn), lambda i, j: (0, j)),
                  pl.BlockSpec((D, tn), lambda i, j: (0, j)),
                  pl.BlockSpec((tm, tn), lambda i, j: (i, ga_off // tn + j)),
                  pl.BlockSpec((tm, tn), lambda i, j: (i, gb_off // tn + j))],
        out_specs=pl.BlockSpec((tm, tn), lambda i, j: (i, j)),
        out_shape=jax.ShapeDtypeStruct((T, N), BF16),
        compiler_params=_cparams("arbitrary", "arbitrary"),
        name="merge",
    )(ha, hb, wa, wb, proj, proj)


def _out_proj_kernel(mix_ref, x_ref, wo_ref, nw_ref, wrh_ref, wrl_ref, br_ref, h_ref, xn_ref, idx_ref, gate_ref):
    h = x_ref[...] + _dot(mix_ref[...], wo_ref[...])
    h_ref[...] = h
    xn = _rms(h, nw_ref[...])
    xn_ref[...] = xn
    xn_hi = xn.astype(BF16)
    xn_lo = (xn - xn_hi.astype(F32)).astype(BF16)
    logits = (_dot(xn_hi, wrh_ref[...]) + (_dot(xn_hi, wrl_ref[...]) + _dot(xn_lo, wrh_ref[...]))
              + br_ref[...])
    tm = logits.shape[0]
    lane_i = lax.broadcasted_iota(jnp.int32, (tm, LANES), 1)
    lane = lane_i.astype(F32)
    vals, idxs = [], []
    for _ in range(TOP_K):
        mx = jnp.max(logits, axis=-1, keepdims=True)
        ix = jnp.min(jnp.where(logits == mx, lane, float(LANES)), axis=-1, keepdims=True)
        vals.append(mx)
        idxs.append(ix)
        logits = jnp.where(lane == ix, -jnp.inf, logits)
    es = [jnp.exp(v - vals[0]) for v in vals]
    tot = es[0]
    for e in es[1:]:
        tot = tot + e
    idx_out = jnp.zeros((tm, LANES), F32)
    gate_out = jnp.zeros((tm, LANES), F32)
    for k in range(TOP_K):
        idx_out = jnp.where(lane_i == k, idxs[k], idx_out)
        gate_out = jnp.where(lane_i == k, es[k] / tot, gate_out)
    idx_ref[...] = idx_out.astype(jnp.int32)
    gate_ref[...] = gate_out


def _out_proj(mixed, x2d, w_out, norm_w, w_router, b_router):
    T, D = x2d.shape
    tm = min(512, T)
    row = lambda i: (i, 0)
    const = lambda i: (0, 0)
    wr_hi = w_router.astype(BF16)
    wr_lo = (w_router - wr_hi.astype(F32)).astype(BF16)
    return pl.pallas_call(
        _out_proj_kernel,
        grid=(T // tm,),
        in_specs=[pl.BlockSpec((tm, D), row), pl.BlockSpec((tm, D), row),
                  pl.BlockSpec((D, D), const), pl.BlockSpec((1, D), const),
                  pl.BlockSpec((D, LANES), const), pl.BlockSpec((D, LANES), const),
                  pl.BlockSpec((1, LANES), const)],
        out_specs=[pl.BlockSpec((tm, D), row), pl.BlockSpec((tm, D), row),
                   pl.BlockSpec((tm, LANES), row), pl.BlockSpec((tm, LANES), row)],
        out_shape=[jax.ShapeDtypeStruct((T, D), F32), jax.ShapeDtypeStruct((T, D), F32),
                   jax.ShapeDtypeStruct((T, LANES), jnp.int32), jax.ShapeDtypeStruct((T, LANES), F32)],
        compiler_params=_cparams("arbitrary"),
        name="out_proj",
    )(mixed, x2d, w_out, norm_w, wr_hi, wr_lo, b_router)


def _gather_copy(src_hbm, dst_buf, sem, src_row, dst_row):
    return pltpu.make_async_copy(src_hbm.at[pl.ds(src_row, 1), :], dst_buf.at[pl.ds(dst_row, 1), :], sem)


def _expert_kernel(be_ref, nh_ref, nv_ref, tok_ref, tokn_ref, x_hbm, wg_ref, wl_ref, bg_ref, bl_ref, wd_ref, bd_ref,
                   o_ref, xbuf, xb, act, sem):
    i = pl.program_id(0)
    c = pl.program_id(1)
    n_valid = nv_ref[0]
    halves = nh_ref[i]
    rows = xb.shape[0]
    half = rows // MOE_HALVES
    nu = act.shape[0]
    nd = xb.shape[1] // o_ref.shape[1]
    fc = act.shape[2]

    @pl.when((c == 0) & (i == 0))
    def _():
        def body(r, carry):
            _gather_copy(x_hbm, xbuf, sem, tok_ref[0, 0, r], r).start()
            return carry
        lax.fori_loop(0, rows, body, 0, unroll=8)

    @pl.when((c == 0) & (i <= n_valid))
    def _():
        pltpu.make_async_copy(x_hbm.at[pl.ds(0, rows), :], xbuf, sem).wait()

    @pl.when((c == 0) & (i < n_valid))
    def _():
        xb[...] = xbuf[...].astype(BF16)

    def request_next_rows(first, count):
        dst = xbuf.at[pl.ds(pl.multiple_of(first, 8), count), :]
        for r in range(count):
            _gather_copy(x_hbm, dst, sem, tokn_ref[0, 0, first + r], r).start()

    per_up = rows // nu
    assert nu * per_up == rows and per_up % 8 == 0

    def up(hf):
        rs = slice(hf * half, (hf + 1) * half)
        x = xb[rs, :]
        h_glu = jnp.minimum(_dot(x, wg_ref[...]) + bg_ref[...], SWIGLU_LIMIT)
        h_lin = jnp.clip(_dot(x, wl_ref[...]) + bl_ref[...], -SWIGLU_LIMIT, SWIGLU_LIMIT)
        act[c, rs, :] = (h_glu * _sigmoid(SWIGLU_ALPHA * h_glu) * (h_lin + 1.0)).astype(BF16)

    def down(hf):
        rs = slice(hf * half, (hf + 1) * half)
        y = _dot(act[0, rs, :], wd_ref[0:fc, :])
        for u in range(1, nu):
            y = y + _dot(act[u, rs, :], wd_ref[u * fc:(u + 1) * fc, :])
        o_ref[rs, :] = y + bd_ref[...]

    @pl.when((halves > 0) & (c < nu))
    def _():
        request_next_rows(c * per_up, per_up)
        up(0)

    @pl.when((halves > 0) & (c >= nu))
    def _():
        down(0)

    for hf in range(1, MOE_HALVES):
        @pl.when((halves > hf) & (c < nu))
        def _():
            up(hf)

        @pl.when((halves > hf) & (c >= nu))
        def _():
            down(hf)

        @pl.when((halves <= hf) & (c >= nu))
        def _():
            rs = slice(hf * half, (hf + 1) * half)
            o_ref[rs, :] = jnp.zeros((half, o_ref.shape[1]), o_ref.dtype)

    @pl.when((halves == 0) & (c >= nu))
    def _():
        o_ref[0:half, :] = jnp.zeros((half, o_ref.shape[1]), o_ref.dtype)


def _experts(xp, buf_tok, block_e, block_halves, n_valid, w_up, b_up, w_down, b_down):
    E, D, F2 = w_up.shape
    F = F2 // 2
    rows = MOE_ROWS * MOE_HALVES
    fc = MOE_FF_CHUNK
    dn = MOE_OUT_CHUNK
    nu = F // fc
    nd = D // dn
    nblk = buf_tok.shape[0] // rows
    tok3 = buf_tok.reshape(nblk, 1, rows)
    cu = lambda i, c, nv: jnp.where(i < nv[0], jnp.minimum(c, nu - 1), nu - 1)
    cd = lambda i, c, nv: jnp.where(i < nv[0], jnp.maximum(c - nu, 0), nd - 1)
    co = lambda c: jnp.maximum(c - nu, 0)
    grid_spec = pltpu.PrefetchScalarGridSpec(
        num_scalar_prefetch=3,
        grid=(nblk, nu + nd),
        in_specs=[
            pl.BlockSpec((1, 1, rows), lambda i, c, be, nh, nv: (i, 0, 0), memory_space=pltpu.SMEM),
            pl.BlockSpec((1, 1, rows), lambda i, c, be, nh, nv: (jnp.minimum(i + 1, nblk - 1), 0, 0),
                         memory_space=pltpu.SMEM),
            pl.BlockSpec(memory_space=pl.ANY),
            pl.BlockSpec((None, D, fc), lambda i, c, be, nh, nv: (be[i], 0, cu(i, c, nv))),
            pl.BlockSpec((None, D, fc), lambda i, c, be, nh, nv: (be[i], 0, nu + cu(i, c, nv))),
            pl.BlockSpec((None, 1, fc), lambda i, c, be, nh, nv: (be[i], 0, cu(i, c, nv))),
            pl.BlockSpec((None, 1, fc), lambda i, c, be, nh, nv: (be[i], 0, nu + cu(i, c, nv))),
            pl.BlockSpec((None, F, dn), lambda i, c, be, nh, nv: (be[i], 0, cd(i, c, nv))),
            pl.BlockSpec((None, 1, dn), lambda i, c, be, nh, nv: (be[i], 0, cd(i, c, nv))),
        ],
        out_specs=pl.BlockSpec((rows, dn), lambda i, c, be, nh, nv: (i, co(c))),
        scratch_shapes=[pltpu.VMEM((rows, D), F32), pltpu.VMEM((rows, D), BF16),
                        pltpu.VMEM((nu, rows, fc), BF16), pltpu.SemaphoreType.DMA],
    )
    return pl.pallas_call(
        _expert_kernel,
        grid_spec=grid_spec,
        out_shape=jax.ShapeDtypeStruct((nblk * rows, D), F32),
        compiler_params=_cparams("arbitrary", "arbitrary"),
        name="experts",
    )(block_e, block_halves, n_valid, tok3, tok3, xp, w_up, w_up, b_up, b_up, w_down, b_down)


def _combine_kernel(slot_ref, slotn_ref, y_hbm, h_ref, gate_ref, p_ref, nw_ref, wg_ref, wp_ref, o_ref, ybuf, sem):
    i = pl.program_id(0)
    nsteps = pl.num_programs(0)
    tm = h_ref.shape[0]
    n = TOP_K * tm
    s = i % 2

    def wait_rows(b):
        pltpu.make_async_copy(y_hbm.at[pl.ds(0, n), :], ybuf.at[b], sem.at[b]).wait()

    @pl.when(i == 0)
    def _():
        def body(r, carry):
            _gather_copy(y_hbm, ybuf.at[0], sem.at[0], slot_ref[0, 0, r], r).start()
            return carry
        lax.fori_loop(0, n, body, 0, unroll=8)

    wait_rows(s)
    nxt = ybuf.at[1 - s]
    for r in range(n):
        _gather_copy(y_hbm, nxt, sem.at[1 - s], slotn_ref[0, 0, r], r).start(priority=r % 2)

    h = h_ref[...]
    gates = gate_ref[...]
    for k in range(TOP_K):
        h = h + gates[:, k:k + 1] * ybuf[s, k * tm:(k + 1) * tm, :]
    gate = _sigmoid(_dot(_rms(h, nw_ref[...]).astype(BF16), wg_ref[...]))
    o_ref[...] = h + gate * _dot(p_ref[...].astype(BF16), wp_ref[...])

    @pl.when(i == nsteps - 1)
    def _():
        wait_rows(1 - s)


def _combine(ys, slots, h1, gates, p2d, norm_w, w_gate, w_proj):
    T, D = h1.shape
    PD = p2d.shape[1]
    tm = min(256, T)
    nsteps = T // tm
    slot3 = slots.reshape(nsteps, tm, TOP_K).transpose(0, 2, 1).reshape(nsteps, 1, TOP_K * tm)
    row = lambda i: (i, 0)
    const = lambda i: (0, 0)
    return pl.pallas_call(
        _combine_kernel,
        grid=(nsteps,),
        in_specs=[pl.BlockSpec((1, 1, TOP_K * tm), lambda i: (i, 0, 0), memory_space=pltpu.SMEM),
                  pl.BlockSpec((1, 1, TOP_K * tm), lambda i: (jnp.minimum(i + 1, nsteps - 1), 0, 0),
                               memory_space=pltpu.SMEM),
                  pl.BlockSpec(memory_space=pl.ANY),
                  pl.BlockSpec((tm, D), row), pl.BlockSpec((tm, LANES), row), pl.BlockSpec((tm, PD), row),
                  pl.BlockSpec((1, D), const), pl.BlockSpec((D, D), const), pl.BlockSpec((PD, D), const)],
        out_specs=pl.BlockSpec((tm, D), row),
        out_shape=jax.ShapeDtypeStruct((T, D), F32),
        scratch_shapes=[pltpu.VMEM((2, TOP_K * tm, D), F32), pltpu.SemaphoreType.DMA((2,))],
        compiler_params=_cparams("arbitrary"),
        name="combine",
    )(slot3, slot3, ys, h1, gates, p2d, norm_w, w_gate, w_proj)


def _route_rank_kernel(idx_ref, rank_ref, cnt_ref, base_ref):
    @pl.when(pl.program_id(0) == 0)
    def _():
        base_ref[...] = jnp.zeros_like(base_ref)

    tm = idx_ref.shape[0]
    lane = lax.broadcasted_iota(jnp.int32, (tm, LANES), 1)
    idx = idx_ref[...]
    onehots = [(lane == idx[:, k:k + 1]).astype(F32) for k in range(TOP_K)]
    anyhot = onehots[0]
    for oh in onehots[1:]:
        anyhot = anyhot + oh
    r = lax.broadcasted_iota(jnp.int32, (tm, tm), 0)
    c = lax.broadcasted_iota(jnp.int32, (tm, tm), 1)
    before = (c < r).astype(BF16)
    prior = _dot(before, anyhot.astype(BF16)) + base_ref[...]
    rank = jnp.zeros((tm, LANES), F32)
    for k in range(TOP_K):
        rank = jnp.where(lane == k, jnp.sum(onehots[k] * prior, axis=-1, keepdims=True), rank)
    rank_ref[...] = rank.astype(jnp.int32)
    base_ref[...] += jnp.sum(anyhot, axis=0, keepdims=True)
    cnt_ref[...] = base_ref[...]


def _route_rank(idx_pad):
    T = idx_pad.shape[0]
    tm = min(ROUTE_ROWS, T)
    return pl.pallas_call(
        _route_rank_kernel,
        grid=(T // tm,),
        in_specs=[pl.BlockSpec((tm, LANES), lambda i: (i, 0))],
        out_specs=[pl.BlockSpec((tm, LANES), lambda i: (i, 0)), pl.BlockSpec((1, LANES), lambda i: (0, 0))],
        out_shape=[jax.ShapeDtypeStruct((T, LANES), jnp.int32), jax.ShapeDtypeStruct((1, LANES), F32)],
        scratch_shapes=[pltpu.VMEM((1, LANES), F32)],
        compiler_params=_cparams("arbitrary"),
        name="route_rank",
    )(idx_pad)


def _routing(idx_pad):
    T = idx_pad.shape[0]
    K = TOP_K
    N = T * K
    E = N_EXPERTS
    rows = MOE_ROWS * MOE_HALVES
    rank_pad, counts_pad = _route_rank(idx_pad)
    counts = counts_pad[0, :E].astype(jnp.int32)
    starts = jnp.cumsum(counts) - counts
    padded = (counts + rows - 1) // rows * rows
    pad_ends = jnp.cumsum(padded)
    pad_starts = pad_ends - padded
    top_idx = idx_pad[:, :K]
    expert_ids = jnp.arange(E, dtype=jnp.int32)
    slots = jnp.sum(jnp.where(top_idx[:, :, None] == expert_ids, pad_starts, 0), axis=-1) + rank_pad[:, :K]
    order = jnp.argsort(top_idx.reshape(-1), stable=True).astype(jnp.int32)
    P = N + E * rows
    nblk = P // rows
    blk_start = jnp.arange(nblk, dtype=jnp.int32) * rows
    block_e = jnp.minimum(jnp.sum((pad_ends[None, :] <= blk_start[:, None]).astype(jnp.int32), axis=1), E - 1)
    blk_off = blk_start - jnp.take(pad_starts, block_e)
    blk_cnt = jnp.take(counts, block_e)
    r_slot = blk_off[:, None] + jnp.arange(rows, dtype=jnp.int32)[None, :]
    src = jnp.clip(jnp.take(starts, block_e)[:, None] + r_slot, 0, N - 1)
    buf_tok = jnp.where(r_slot < blk_cnt[:, None], jnp.take(order, src.reshape(-1)).reshape(nblk, rows) // K, 0)
    buf_tok = buf_tok.reshape(-1)
    n_valid = (pad_ends[-1] // rows).astype(jnp.int32).reshape(1)
    remaining = blk_cnt - blk_off
    block_halves = jnp.clip((remaining + MOE_ROWS - 1) // MOE_ROWS, 0, MOE_HALVES).astype(jnp.int32)
    return slots, buf_tok, block_e, block_halves, n_valid


def _layer(h2d, p2d, B, S, norm_mix_w, w_in, ml_igate_b, ml_fgate_b, ml_norm_w, sw_q_norm_w, sw_k_norm_w, sw_sinks,
           w_branch_a, w_branch_b, w_out, norm_ffn_w, w_router, b_router, w_expert_up, b_expert_up,
           w_expert_down, b_expert_down, norm_ple_w, w_ple_gate, w_ple_proj):
    T, D = h2d.shape
    H = ML_HEADS
    qk_w, v_w = H * ML_DQK, H * ML_DV
    sq_w, skv_w = SW_HEADS * SW_HD, SW_KV_HEADS * SW_HD
    g0 = 2 * qk_w + 2 * v_w
    s0 = g0 + 2 * H
    w_sq = w_in[:, s0:s0 + sq_w]
    w_skv = w_in[:, s0 + sq_w:s0 + sq_w + 2 * skv_w]
    w_gab = w_in[:, s0 + sq_w + 2 * skv_w:]
    w_main = jnp.concatenate([w_in[:, :g0], w_sq, w_gab, w_skv], axis=1).astype(BF16)
    w_gate = jnp.pad(w_in[:, g0:s0], ((0, 0), (0, LANES - 2 * H))).astype(BF16)
    sq_off = g0
    ga_off = sq_off + sq_w
    gb_off = ga_off + D
    sk_off = gb_off + D
    sv_off = sk_off + skv_w

    proj, gates = _in_proj(h2d, norm_mix_w.reshape(1, D), w_main, w_gate)

    gate_b = jnp.concatenate([ml_igate_b, ml_fgate_b]).astype(F32)
    bias_row = jnp.pad(gate_b, (0, LANES - 2 * H)).reshape(1, LANES)
    bias_col = gate_b.reshape(2 * H, 1)
    gates_t = gates[:, :2 * H].T
    E = N_EXPERTS
    up_shape, down_shape = w_expert_up.shape, w_expert_down.shape
    h_ml, w_up_bf = _mlstm(proj, gates, gates_t, bias_row, bias_col, ml_norm_w.reshape(1, v_w),
                           w_expert_up.reshape(E * up_shape[1], up_shape[2]), B, S)

    h_sw, w_down_bf = _swa(proj, sw_sinks.astype(F32), sw_q_norm_w.reshape(1, SW_HD), sw_k_norm_w.reshape(1, SW_HD),
                           w_expert_down.reshape(E * down_shape[1], down_shape[2]), B, S, sq_off, sk_off, sv_off)

    mixed = _merge(h_ml, h_sw, w_branch_a.astype(BF16), w_branch_b.astype(BF16), proj, ga_off, gb_off)

    wr = jnp.pad(w_router, ((0, 0), (0, LANES - N_EXPERTS)))
    br = jnp.pad(b_router, (0, LANES - N_EXPERTS), constant_values=NEG_BIG).reshape(1, LANES)
    h1, xn, idx_pad, gate_pad = _out_proj(mixed, h2d, w_out.astype(BF16), norm_ffn_w.reshape(1, D), wr, br)

    slots, buf_tok, block_e, block_halves, n_valid = _routing(idx_pad)
    ys = _experts(xn, buf_tok, block_e, block_halves, n_valid,
                  w_up_bf.reshape(up_shape), b_expert_up.reshape(E, 1, -1),
                  w_down_bf.reshape(down_shape), b_expert_down.reshape(E, 1, -1))

    return _combine(ys, slots, h1, gate_pad, p2d, norm_ple_w.reshape(1, D),
                    w_ple_gate.astype(BF16), w_ple_proj.astype(BF16))


def kernel(x, p, norm_mix_w, w_in, ml_igate_b, ml_fgate_b, ml_norm_w, sw_q_norm_w, sw_k_norm_w, sw_sinks,
           w_branch_a, w_branch_b, w_out, norm_ffn_w, w_router, b_router, w_expert_up, b_expert_up,
           w_expert_down, b_expert_down, norm_ple_w, w_ple_gate, w_ple_proj):
    B, S, D = x.shape
    depth = p.shape[0]
    h = x.reshape(B * S, D)
    for i in range(depth):
        h = _layer(h, p[i].reshape(B * S, -1), B, S, norm_mix_w[i], w_in[i], ml_igate_b[i], ml_fgate_b[i],
                   ml_norm_w[i], sw_q_norm_w[i], sw_k_norm_w[i], sw_sinks[i], w_branch_a[i], w_branch_b[i],
                   w_out[i], norm_ffn_w[i], w_router[i], b_router[i], w_expert_up[i], b_expert_up[i],
                   w_expert_down[i], b_expert_down[i], norm_ple_w[i], w_ple_gate[i], w_ple_proj[i])
    return h.reshape(B, S, D)
```
